```python
import jax, jax.numpy as jnp
from jax import lax
import numpy as np

D_MODEL = 1024
BATCH = 8
SEQ = 8192
DEPTH = 1

D_MIX = D_MODEL
GDN_HEADS = 4
GDN_HEAD_DIM = 128
GDN_WIDTH = GDN_HEADS * GDN_HEAD_DIM
POOL_WINDOWS = (2, 4, 8, 16)
POOL_GROUPS = len(POOL_WINDOWS)
POOL_WIDTH = D_MIX - GDN_WIDTH
POOL_GROUP_DIM = POOL_WIDTH // POOL_GROUPS
CONV_K = 4
CHUNK = 64
D_FF = ((8 * D_MODEL // 3 + 255) // 256) * 256
D_IN = 4 * GDN_WIDTH + 2 * GDN_HEADS + POOL_WIDTH
N_MOD = 9
EPS = 1e-6

kernel_name = "hybrid_gdn_pool_macaron_adaln"


def rms_norm(x, gain):
    xf = x.astype(jnp.float32)
    y = xf * lax.rsqrt(jnp.mean(xf * xf, axis=-1, keepdims=True) + EPS)
    return (y * gain.astype(jnp.float32)).astype(x.dtype)


def l2_normalize(x):
    xf = x.astype(jnp.float32)
    return xf * lax.rsqrt(jnp.sum(xf * xf, axis=-1, keepdims=True) + EPS)


def modulate(h, shift, scale):
    return h * (1 + scale[:, None, :]) + shift[:, None, :]


def swiglu(h, w_gate, w_up, w_down):
    return (jax.nn.silu(h @ w_gate) * (h @ w_up)) @ w_down


def causal_depthwise_conv_silu(x, w):
    C = x.shape[-1]
    y = lax.conv_general_dilated(
        x, w[:, None, :].astype(x.dtype), window_strides=(1,), padding=[(CONV_K - 1, 0)],
        dimension_numbers=("NWC", "WIO", "NWC"), feature_group_count=C)
    return jax.nn.silu(y)


def gated_delta_rule_chunked(q, k, v, g, beta):
    B, T, H, Dk = q.shape
    Dv = v.shape[-1]
    N = T // CHUNK

    def chunks(t):
        t = t.reshape((B, N, CHUNK, H) + t.shape[3:])
        return jnp.moveaxis(t, 3, 1)

    q, k, v, g, beta = chunks(q), chunks(k), chunks(v), chunks(g), chunks(beta)
    g_cum = jnp.cumsum(g, axis=-1)
    causal = jnp.tril(jnp.ones((CHUNK, CHUNK), dtype=bool))
    strict = jnp.tril(jnp.ones((CHUNK, CHUNK), dtype=bool), -1)
    diff = g_cum[..., :, None] - g_cum[..., None, :]
    decay = jnp.where(causal, jnp.exp(jnp.where(causal, diff, 0.0)), 0.0)
    k_beta = k * beta[..., None]
    m = jnp.where(strict, jnp.einsum("bhnid,bhnjd->bhnij", k_beta, k) * decay, 0.0)
    a = m + jnp.eye(CHUNK, dtype=m.dtype)
    u = lax.linalg.triangular_solve(a, v * beta[..., None], left_side=True, lower=True)
    w = lax.linalg.triangular_solve(a, k_beta * jnp.exp(g_cum)[..., None], left_side=True, lower=True)
    intra = jnp.einsum("bhnid,bhnjd->bhnij", q, k) * decay
    g_last = g_cum[..., -1:]
    q_dec = q * jnp.exp(g_cum)[..., None]
    k_dec = k * jnp.exp(g_last - g_cum)[..., None]
    chunk_decay = jnp.exp(g_last[..., 0])
    xs = tuple(jnp.moveaxis(t, 2, 0) for t in (q_dec, k_dec, u, w, intra, chunk_decay))

    def step(S, inp):
        qd, kd, un, wn, an, cd = inp
        v_new = un - jnp.einsum("bhck,bhkv->bhcv", wn, S)
        o = jnp.einsum("bhck,bhkv->bhcv", qd, S) + jnp.einsum("bhij,bhjv->bhiv", an, v_new)
        S = S * cd[..., None, None] + jnp.einsum("bhck,bhcv->bhkv", kd, v_new)
        return S, o

    S0 = jnp.zeros((B, H, Dk, Dv), jnp.float32)
    _, o = lax.scan(step, S0, xs)
    return jnp.transpose(o, (1, 0, 3, 2, 4)).reshape(B, T, H, Dv)


def multiscale_causal_pool(p):
    B, T, _ = p.shape
    pf = p.astype(jnp.float32).reshape(B, T, POOL_GROUPS, POOL_GROUP_DIM)
    cs0 = jnp.concatenate([jnp.zeros_like(pf[:, :1]), jnp.cumsum(pf, axis=1)], axis=1)
    t1 = jnp.arange(1, T + 1, dtype=jnp.float32)
    outs = []
    for gi, win in enumerate(POOL_WINDOWS):
        cur = cs0[:, 1:, gi]
        lag = jnp.concatenate(
            [jnp.zeros((B, win - 1, POOL_GROUP_DIM), jnp.float32), cs0[:, :T - win + 1, gi]], axis=1)
        cnt = jnp.minimum(t1, win)[None, :, None]
        outs.append((cur - lag) / cnt - pf[:, :, gi])
    return jnp.stack(outs, axis=2)


def setup_inputs(seed: int = 0) -> dict:
    key = jax.random.key(seed)
    ks = jax.random.split(key, 24)
    f32 = jnp.float32
    L, D = DEPTH, D_MODEL

    def nrm(k, shape, scale):
        return jax.random.normal(k, shape, f32) * scale

    def gain(k, shape):
        return 1.0 + 0.02 * jax.random.normal(k, shape, f32)

    dt = jnp.exp(jax.random.uniform(ks[11], (L, GDN_HEADS), f32, np.log(1e-3), np.log(1e-1)))
    return {
        "x": nrm(ks[0], (BATCH, SEQ, D), 1.0),
        "c": nrm(ks[1], (BATCH, D), 1.0),
        "w_ada": nrm(ks[2], (L, D, N_MOD * D), 0.5 * D ** -0.5),
        "b_ada": nrm(ks[3], (L, N_MOD * D), 0.01),
        "norm_ffn1": gain(ks[4], (L, D)),
        "ffn1_gate": nrm(ks[5], (L, D, D_FF), D ** -0.5),
        "ffn1_up": nrm(ks[6], (L, D, D_FF), D ** -0.5),
        "ffn1_down": nrm(ks[7], (L, D_FF, D), D_FF ** -0.5),
        "norm_mix": gain(ks[8], (L, D)),
        "w_in": nrm(ks[9], (L, D, D_IN), D ** -0.5),
        "conv_w": nrm(ks[10], (L, CONV_K, 3 * GDN_WIDTH), CONV_K ** -0.5),
        "a_log": jnp.log(jax.random.uniform(ks[12], (L, GDN_HEADS), f32, 1.0, 16.0)),
        "dt_bias": dt + jnp.log(-jnp.expm1(-dt)),
        "gdn_norm": gain(ks[13], (L, GDN_HEAD_DIM)),
        "pool_w": nrm(ks[14], (L, POOL_GROUPS, POOL_GROUP_DIM, POOL_GROUP_DIM), POOL_GROUP_DIM ** -0.5),
        "pool_scale": gain(ks[15], (L, POOL_WIDTH)),
        "w_out": nrm(ks[16], (L, D_MIX, D), D_MIX ** -0.5),
        "norm_ffn2": gain(ks[17], (L, D)),
        "ffn2_gate": nrm(ks[18], (L, D, D_FF), D ** -0.5),
        "ffn2_up": nrm(ks[19], (L, D, D_FF), D ** -0.5),
        "ffn2_down": nrm(ks[20], (L, D_FF, D), D_FF ** -0.5),
        "final_norm": gain(ks[21], (D,)),
    }


def reference(x, c, w_ada, b_ada, norm_ffn1, ffn1_gate, ffn1_up, ffn1_down, norm_mix, w_in, conv_w,
              a_log, dt_bias, gdn_norm, pool_w, pool_scale, w_out, norm_ffn2, ffn2_gate, ffn2_up,
              ffn2_down, final_norm):
    B, T, _ = x.shape
    H, Dh, GW = GDN_HEADS, GDN_HEAD_DIM, GDN_WIDTH
    split_at = [3 * GW, 4 * GW, 4 * GW + H, 4 * GW + 2 * H]
    for l in range(DEPTH):
        mod = (jax.nn.silu(c) @ w_ada[l] + b_ada[l]).reshape(B, N_MOD, D_MODEL)

        h = modulate(rms_norm(x, norm_ffn1[l]), mod[:, 0], mod[:, 1])
        x = x + 0.5 * mod[:, 2][:, None, :] * swiglu(h, ffn1_gate[l], ffn1_up[l], ffn1_down[l])

        h = modulate(rms_norm(x, norm_mix[l]), mod[:, 3], mod[:, 4])
        proj = h @ w_in[l]
        qkv, z, b_raw, a_raw, p = jnp.split(proj, split_at, axis=-1)
        qkv = causal_depthwise_conv_silu(qkv, conv_w[l])
        q, k, v = jnp.split(qkv, 3, axis=-1)
        q = l2_normalize(q.reshape(B, T, H, Dh)) * (Dh ** -0.5)
        k = l2_normalize(k.reshape(B, T, H, Dh))
        v = v.reshape(B, T, H, Dh).astype(jnp.float32)
        beta = jax.nn.sigmoid(b_raw.astype(jnp.float32))
        g = -jnp.exp(a_log[l].astype(jnp.float32)) * jax.nn.softplus(
            a_raw.astype(jnp.float32) + dt_bias[l].astype(jnp.float32))
        o = gated_delta_rule_chunked(q, k, v, g, beta)
        o = rms_norm(o, gdn_norm[l]) * jax.nn.silu(z.reshape(B, T, H, Dh).astype(jnp.float32))
        gdn_out = o.reshape(B, T, GW).astype(x.dtype)

        pooled = multiscale_causal_pool(p)
        pooled = jnp.einsum("btgi,gio->btgo", pooled, pool_w[l].astype(jnp.float32))
        pool_out = (pooled.reshape(B, T, POOL_WIDTH) * pool_scale[l].astype(jnp.float32)).astype(x.dtype)

        mixed = jnp.concatenate([gdn_out, pool_out], axis=-1) @ w_out[l]
        x = x + mod[:, 5][:, None, :] * mixed

        h = modulate(rms_norm(x, norm_ffn2[l]), mod[:, 6], mod[:, 7])
        x = x + 0.5 * mod[:, 8][:, None, :] * swiglu(h, ffn2_gate[l], ffn2_up[l], ffn2_down[l])
    return rms_norm(x, final_norm)
```

```python
import functools

import jax
import jax.numpy as jnp
from jax import lax
from jax.experimental import pallas as pl
from jax.experimental.pallas import tpu as pltpu

F32 = jnp.float32
BF16 = jnp.bfloat16

D_MODEL = 1024
GDN_HEADS = 4
HEAD_DIM = 128
GDN_WIDTH = GDN_HEADS * HEAD_DIM
POOL_WINDOWS = (2, 4, 8, 16)
POOL_GROUPS = len(POOL_WINDOWS)
POOL_GROUP_DIM = 128
POOL_WIDTH = POOL_GROUPS * POOL_GROUP_DIM
CONV_K = 4
D_FF = 2816
N_MOD = 9
EPS = 1e-6

LANES = 128
SUBLANES = 8
VMEM_LIMIT_BYTES = 56 * 1024 * 1024

GDN_CHUNK = LANES
INV_BASE = 16
CONV_HALO = SUBLANES
POOL_HALO = 16
NEG_BIG = -1e30

FFN_ROWS = 512
MIX_ROWS = 512


def _dot(a, b):
    return jnp.dot(a, b, preferred_element_type=F32)


def _dot_nt(a, b):
    return lax.dot_general(a, b, (((1,), (1,)), ((), ())), preferred_element_type=F32)


def _dot_tn(a, b):
    return lax.dot_general(a, b, (((0,), (0,)), ((), ())), preferred_element_type=F32)


def _silu(x):
    return x * jax.nn.sigmoid(x)


def _norm_mod(x, gain, shift, scale):
    ms = jnp.mean(x * x, axis=-1, keepdims=True)
    y = x * lax.rsqrt(ms + EPS) * gain
    return y * (1.0 + scale) + shift


def _ada_kernel(c_ref, w_ref, b_ref, o_ref):
    a = _silu(c_ref[...])
    o_ref[...] = jnp.dot(a, w_ref[...], preferred_element_type=F32,
                         precision=lax.Precision.HIGHEST) + b_ref[...]


def _ada_call(c, w_ada, b_ada):
    B = c.shape[0]
    n = w_ada.shape[1]
    bn = D_MODEL
    return pl.pallas_call(
        _ada_kernel,
        grid=(n // bn,),
        in_specs=[
            pl.BlockSpec((B, D_MODEL), lambda j: (0, 0)),
            pl.BlockSpec((D_MODEL, bn), lambda j: (0, j)),
            pl.BlockSpec((1, bn), lambda j: (0, j)),
        ],
        out_specs=pl.BlockSpec((B, bn), lambda j: (0, j)),
        out_shape=jax.ShapeDtypeStruct((B, n), F32),
        compiler_params=pltpu.CompilerParams(dimension_semantics=("arbitrary",)),
        name="adaln_mod",
    )(c, w_ada, b_ada.reshape(1, n))


def _ffn_kernel(x_ref, mod_ref, gain_ref, wg_ref, wu_ref, wd_ref, fgain_ref, o_ref, *,
                mod_row, final_norm):
    x = x_ref[...]
    shift = mod_ref[0, mod_row:mod_row + 1, :]
    scale = mod_ref[0, mod_row + 1:mod_row + 2, :]
    gate = mod_ref[0, mod_row + 2:mod_row + 3, :]
    h = _norm_mod(x, gain_ref[...], shift, scale).astype(BF16)
    g = _dot(h, wg_ref[...])
    u = _dot(h, wu_ref[...])
    a = (_silu(g) * u).astype(BF16)
    d = _dot(a, wd_ref[...])
    y = x + (0.5 * gate) * d
    if final_norm:
        ms = jnp.mean(y * y, axis=-1, keepdims=True)
        y = y * lax.rsqrt(ms + EPS) * fgain_ref[...]
    o_ref[...] = y


def _resident(shape):
    nd = len(shape)
    return pl.BlockSpec(shape, lambda *_: (0,) * nd, pipeline_mode=pl.Buffered(1))


def _ffn_call(x2d, mod, gain, wg, wu, wd, fgain, *, mod_row, final_norm, seq):
    n_rows = x2d.shape[0]
    tm = FFN_ROWS
    steps_per_seq = seq // tm
    kern = functools.partial(_ffn_kernel, mod_row=mod_row, final_norm=final_norm)
    return pl.pallas_call(
        kern,
        grid=(n_rows // tm,),
        in_specs=[
            pl.BlockSpec((tm, D_MODEL), lambda i: (i, 0)),
            pl.BlockSpec((1, N_MOD, D_MODEL), lambda i: (i // steps_per_seq, 0, 0)),
            _resident((1, D_MODEL)),
            _resident((D_MODEL, D_FF)),
            _resident((D_MODEL, D_FF)),
            _resident((D_FF, D_MODEL)),
            _resident((1, D_MODEL)),
        ],
        out_specs=pl.BlockSpec((tm, D_MODEL), lambda i: (i, 0)),
        out_shape=jax.ShapeDtypeStruct((n_rows, D_MODEL), F32),
        compiler_params=pltpu.CompilerParams(
            dimension_semantics=("arbitrary",), vmem_limit_bytes=VMEM_LIMIT_BYTES),
        name="ffn_final" if final_norm else "ffn",
    )(x2d, mod, gain, wg, wu, wd, fgain)


def _inverse_masks(c):
    row = lax.broadcasted_iota(jnp.int32, (c, c), 0)
    col = lax.broadcasted_iota(jnp.int32, (c, c), 1)
    base_shift = INV_BASE.bit_length() - 1
    eye = (row == col).astype(F32)
    same_base = (row >> base_shift) == (col >> base_shift)
    offs = []
    sh = base_shift
    while (1 << sh) < c:
        offs.append(((row >> (sh + 1)) == (col >> (sh + 1))) & ((row >> sh) != (col >> sh)))
        sh += 1
    return eye, same_base, offs


def _tri_inverse(m, masks):
    c = m.shape[0]
    eye, same_base, offs = masks
    n = jnp.where(same_base, -m, 0.0)
    x = eye + n
    nb = n.astype(BF16)
    p = _dot(nb, nb)
    size = 2
    while size < INV_BASE:
        pb = p.astype(BF16)
        if 2 * size < INV_BASE:
            xp = _dot(jnp.concatenate([x, p], axis=0).astype(BF16), pb)
            x = x + xp[:c]
            p = xp[c:]
        else:
            x = x + _dot(x.astype(BF16), pb)
        size *= 2
    for off in offs:
        m_off = jnp.where(off, m, 0.0).astype(BF16)
        xb = x.astype(BF16)
        x = x - _dot(xb, _dot(m_off, xb).astype(BF16))
    return x


def _mixer_kernel(x_ref, mod_ref, gain_ref, wqkvz_ref, wba_ref, wp_ref, convw_ref, hp_ref,
                  gnorm_ref, poolw_ref, pscale_ref, wout_ref, o_ref,
                  s_ref, convh_ref, poolh_ref, cat_ref):
    tm = x_ref.shape[1]
    t_blk = pl.program_id(1)

    @pl.when(t_blk == 0)
    def _():
        s_ref[...] = jnp.zeros_like(s_ref)
        convh_ref[...] = jnp.zeros_like(convh_ref)
        poolh_ref[...] = jnp.zeros_like(poolh_ref)

    x = x_ref[0]
    shift = mod_ref[0, 3:4, :]
    scale = mod_ref[0, 4:5, :]
    gate = mod_ref[0, 5:6, :]
    h = _norm_mod(x, gain_ref[...], shift, scale).astype(BF16)

    qkvz = _dot(h, wqkvz_ref[...])
    ba = _dot(h, wba_ref[...])
    p = _dot(h, wp_ref[...])

    qkv = qkvz[:, :3 * GDN_WIDTH]
    z = qkvz[:, 3 * GDN_WIDTH:]
    halo = convh_ref[...]
    convh_ref[...] = qkv[tm - CONV_HALO:, :]
    row8 = lax.broadcasted_iota(jnp.int32, (CONV_HALO, 1), 0)
    acc = qkv * convw_ref[CONV_K - 1:CONV_K, :]
    for s in range(1, CONV_K):
        r = pltpu.roll(qkv, s, axis=0)
        hr = pltpu.roll(halo, s, axis=0)
        top = jnp.where(row8 < s, hr, r[:CONV_HALO])
        r = jnp.concatenate([top, r[CONV_HALO:]], axis=0)
        acc = acc + r * convw_ref[CONV_K - 1 - s:CONV_K - s, :]
    qkv = _silu(acc)

    beta_all = jax.nn.sigmoid(ba)
    a_shift = ba + hp_ref[1:2, :]
    softplus = jnp.maximum(a_shift, 0.0) + jnp.log1p(jnp.exp(-jnp.abs(a_shift)))
    g_all = -jnp.exp(hp_ref[0:1, :]) * softplus
    rows = lax.broadcasted_iota(jnp.int32, (tm, 1), 0)
    row_in_chunk = rows & (GDN_CHUNK - 1)
    gc = g_all
    step = 1
    while step < GDN_CHUNK:
        gc = gc + jnp.where(row_in_chunk >= step, pltpu.roll(gc, step, axis=0), 0.0)
        step *= 2
    gc_t = gc.T

    ri = lax.broadcasted_iota(jnp.int32, (GDN_CHUNK, GDN_CHUNK), 0)
    ci = lax.broadcasted_iota(jnp.int32, (GDN_CHUNK, GDN_CHUNK), 1)
    causal = ri >= ci
    strict = ri > ci
    gnorm = gnorm_ref[...]
    inv_masks = _inverse_masks(GDN_CHUNK)

    for hd in range(GDN_HEADS):
        lo, hi = hd * HEAD_DIM, (hd + 1) * HEAD_DIM
        q_h = qkv[:, lo:hi]
        k_h = qkv[:, GDN_WIDTH + lo:GDN_WIDTH + hi]
        v_h = qkv[:, 2 * GDN_WIDTH + lo:2 * GDN_WIDTH + hi]
        q_h = q_h * (lax.rsqrt(jnp.sum(q_h * q_h, axis=-1, keepdims=True) + EPS) * HEAD_DIM ** -0.5)
        k_h = k_h * lax.rsqrt(jnp.sum(k_h * k_h, axis=-1, keepdims=True) + EPS)
        z_h = z[:, lo:hi]
        state = s_ref[hd]
        for c in range(tm // GDN_CHUNK):
            r0, r1 = c * GDN_CHUNK, (c + 1) * GDN_CHUNK
            q_c, k_c, v_c = q_h[r0:r1], k_h[r0:r1], v_h[r0:r1]
            g_col = gc[r0:r1, GDN_HEADS + hd:GDN_HEADS + hd + 1]
            g_row = gc_t[GDN_HEADS + hd:GDN_HEADS + hd + 1, r0:r1]
            beta = beta_all[r0:r1, hd:hd + 1]
            decay = jnp.exp(jnp.where(causal, g_col - g_row, NEG_BIG))
            e_g = jnp.exp(g_col)
            g_last = g_col[GDN_CHUNK - 1:GDN_CHUNK, :]
            e_k = jnp.exp(g_last - g_col)
            e_last = jnp.exp(g_last)

            kb = k_c * beta
            k_bf = k_c.astype(BF16)
            kq = _dot_nt(jnp.concatenate([kb, q_c], axis=0).astype(BF16), k_bf)
            m = jnp.where(strict, kq[:GDN_CHUNK] * decay, 0.0)
            intra = kq[GDN_CHUNK:] * decay
            t_inv = _tri_inverse(m, inv_masks)
            uw = _dot(t_inv.astype(BF16),
                      jnp.concatenate([v_c * beta, kb * e_g], axis=1).astype(BF16))
            state_bf = state.astype(BF16)
            v_new = uw[:, :HEAD_DIM] - _dot(uw[:, HEAD_DIM:].astype(BF16), state_bf)
            v_new_bf = v_new.astype(BF16)
            o = _dot(jnp.concatenate([q_c * e_g, intra], axis=1).astype(BF16),
                     jnp.concatenate([state_bf, v_new_bf], axis=0))
            state = state * e_last + _dot_tn((k_c * e_k).astype(BF16), v_new_bf)

            o = o * lax.rsqrt(jnp.mean(o * o, axis=-1, keepdims=True) + EPS) * gnorm
            o = o * _silu(z_h[r0:r1])
            cat_ref[r0:r1, lo:hi] = o.astype(BF16)
        s_ref[hd] = state

    ext = jnp.concatenate([poolh_ref[...], p], axis=0)
    poolh_ref[...] = p[tm - POOL_HALO:, :]
    t_glob = (t_blk * tm + rows + 1).astype(F32)
    for gi, win in enumerate(POOL_WINDOWS):
        lo, hi = gi * POOL_GROUP_DIM, (gi + 1) * POOL_GROUP_DIM
        e = ext[:, lo:hi]
        ws = e
        sh = 1
        while sh < win:
            ws = ws + pltpu.roll(ws, sh, axis=0)
            sh *= 2
        cnt = jnp.minimum(t_glob, float(win))
        pooled = ws[POOL_HALO:] / cnt - p[:, lo:hi]
        po = _dot(pooled.astype(BF16), poolw_ref[gi]) * pscale_ref[:, lo:hi]
        cat_ref[:, GDN_WIDTH + lo:GDN_WIDTH + hi] = po.astype(BF16)

    mixed = _dot(cat_ref[...], wout_ref[...])
    o_ref[0] = x + gate * mixed


def _mixer_call(x, mod, gain, wqkvz, wba, wp, convw, hp, gnorm, poolw, pscale, wout):
    B, T, _ = x.shape
    tm = MIX_ROWS
    return pl.pallas_call(
        _mixer_kernel,
        grid=(B, T // tm),
        in_specs=[
            pl.BlockSpec((1, tm, D_MODEL), lambda b, t: (b, t, 0)),
            pl.BlockSpec((1, N_MOD, D_MODEL), lambda b, t: (b, 0, 0)),
            _resident((1, D_MODEL)),
            _resident(wqkvz.shape),
            _resident(wba.shape),
            _resident(wp.shape),
            _resident(convw.shape),
            _resident(hp.shape),
            _resident(gnorm.shape),
            _resident(poolw.shape),
            _resident(pscale.shape),
            _resident(wout.shape),
        ],
        out_specs=pl.BlockSpec((1, tm, D_MODEL), lambda b, t: (b, t, 0)),
        out_shape=jax.ShapeDtypeStruct((B, T, D_MODEL), F32),
        scratch_shapes=[
            pltpu.VMEM((GDN_HEADS, HEAD_DIM, HEAD_DIM), F32),
            pltpu.VMEM((CONV_HALO, 3 * GDN_WIDTH), F32),
            pltpu.VMEM((POOL_HALO, POOL_WIDTH), F32),
            pltpu.VMEM((tm, D_MODEL), BF16),
        ],
        compiler_params=pltpu.CompilerParams(
            dimension_semantics=("arbitrary", "arbitrary"), vmem_limit_bytes=VMEM_LIMIT_BYTES),
        name="mixer",
    )(x, mod, gain, wqkvz, wba, wp, convw, hp, gnorm, poolw, pscale, wout)


def kernel(x, c, w_ada, b_ada, norm_ffn1, ffn1_gate, ffn1_up, ffn1_down, norm_mix, w_in, conv_w,
           a_log, dt_bias, gdn_norm, pool_w, pool_scale, w_out, norm_ffn2, ffn2_gate, ffn2_up,
           ffn2_down, final_norm):
    B, T, D = x.shape
    depth = w_ada.shape[0]
    H, GW = GDN_HEADS, GDN_WIDTH
    fgain = final_norm.reshape(1, D)
    for l in range(depth):
        mod = _ada_call(c, w_ada[l], b_ada[l]).reshape(B, N_MOD, D)

        last = l == depth - 1
        x2d = _ffn_call(x.reshape(B * T, D), mod, norm_ffn1[l].reshape(1, D),
                        ffn1_gate[l].astype(BF16), ffn1_up[l].astype(BF16),
                        ffn1_down[l].astype(BF16), fgain, mod_row=0, final_norm=False, seq=T)

        wi = w_in[l]
        wqkvz = wi[:, :4 * GW].astype(BF16)
        wba = jnp.pad(wi[:, 4 * GW:4 * GW + 2 * H], ((0, 0), (0, LANES - 2 * H))).astype(BF16)
        wp = wi[:, 4 * GW + 2 * H:].astype(BF16)
        hp = jnp.zeros((2, LANES), F32)
        hp = hp.at[0, H:2 * H].set(a_log[l]).at[1, H:2 * H].set(dt_bias[l])
        x3d = _mixer_call(x2d.reshape(B, T, D), mod, norm_mix[l].reshape(1, D), wqkvz, wba, wp,
                          conv_w[l], hp, gdn_norm[l].reshape(1, HEAD_DIM),
                          pool_w[l].astype(BF16), pool_scale[l].reshape(1, POOL_WIDTH),
                          w_out[l].astype(BF16))

        x2d = _ffn_call(x3d.reshape(B * T, D), mod, norm_ffn2[l].reshape(1, D),
                        ffn2_gate[l].astype(BF16), ffn2_up[l].astype(BF16),
                        ffn2_down[l].astype(BF16), fgain, mod_row=6, final_norm=last, seq=T)
        x = x2d.reshape(B, T, D)
    return x
```

```python
import functools

import jax
import jax.numpy as jnp
from jax import lax
from jax.experimental import pallas as pl
from jax.experimental.pallas import tpu as pltpu

F32 = jnp.float32
BF16 = jnp.bfloat16

D_MODEL = 1024
GDN_HEADS = 4
HEAD_DIM = 128
GDN_WIDTH = GDN_HEADS * HEAD_DIM
POOL_WINDOWS = (2, 4, 8, 16)
POOL_GROUPS = len(POOL_WINDOWS)
POOL_GROUP_DIM = 128
POOL_WIDTH = POOL_GROUPS * POOL_GROUP_DIM
CONV_K = 4
D_FF = 2816
N_MOD = 9
EPS = 1e-6

LANES = 128
SUBLANES = 8
VMEM_LIMIT_BYTES = 56 * 1024 * 1024

GDN_CHUNK = LANES
INV_BASE = 16
CONV_HALO = SUBLANES
POOL_HALO = 16
NEG_BIG = -1e30

FFN_ROWS = 512
MIX_ROWS = 512


def _dot(a, b):
    return jnp.dot(a, b, preferred_element_type=F32)


def _dot_nt(a, b):
    return lax.dot_general(a, b, (((1,), (1,)), ((), ())), preferred_element_type=F32)


def _dot_tn(a, b):
    return lax.dot_general(a, b, (((0,), (0,)), ((), ())), preferred_element_type=F32)


def _silu(x):
    return x * jax.nn.sigmoid(x)


def _norm_mod(x, gain, shift, scale):
    ms = jnp.mean(x * x, axis=-1, keepdims=True)
    y = x * lax.rsqrt(ms + EPS) * gain
    return y * (1.0 + scale) + shift


def _ada_kernel(c_ref, w_ref, b_ref, o_ref):
    a = _silu(c_ref[...])
    o_ref[...] = jnp.dot(a, w_ref[...], preferred_element_type=F32,
                         precision=lax.Precision.HIGHEST) + b_ref[...]


def _ada_call(c, w_ada, b_ada):
    B = c.shape[0]
    n = w_ada.shape[1]
    bn = D_MODEL
    return pl.pallas_call(
        _ada_kernel,
        grid=(n // bn,),
        in_specs=[
            pl.BlockSpec((B, D_MODEL), lambda j: (0, 0)),
            pl.BlockSpec((D_MODEL, bn), lambda j: (0, j)),
            pl.BlockSpec((1, bn), lambda j: (0, j)),
        ],
        out_specs=pl.BlockSpec((B, bn), lambda j: (0, j)),
        out_shape=jax.ShapeDtypeStruct((B, n), F32),
        compiler_params=pltpu.CompilerParams(dimension_semantics=("arbitrary",)),
        name="adaln_mod",
    )(c, w_ada, b_ada.reshape(1, n))


def _ffn_kernel(x_ref, mod_ref, gain_ref, wg_ref, wu_ref, wd_ref, fgain_ref, o_ref, *,
                mod_row, final_norm):
    x = x_ref[...]
    shift = mod_ref[0, mod_row:mod_row + 1, :]
    scale = mod_ref[0, mod_row + 1:mod_row + 2, :]
    gate = mod_ref[0, mod_row + 2:mod_row + 3, :]
    h = _norm_mod(x, gain_ref[...], shift, scale).astype(BF16)
    g = _dot(h, wg_ref[...])
    u = _dot(h, wu_ref[...])
    a = (_silu(g) * u).astype(BF16)
    d = _dot(a, wd_ref[...])
    y = x + (0.5 * gate) * d
    if final_norm:
        ms = jnp.mean(y * y, axis=-1, keepdims=True)
        y = y * lax.rsqrt(ms + EPS) * fgain_ref[...]
    o_ref[...] = y


def _resident(shape):
    nd = len(shape)
    return pl.BlockSpec(shape, lambda *_: (0,) * nd, pipeline_mode=pl.Buffered(1))


def _ffn_call(x2d, mod, gain, wg, wu, wd, fgain, *, mod_row, final_norm, seq):
    n_rows = x2d.shape[0]
    tm = FFN_ROWS
    steps_per_seq = seq // tm
    kern = functools.partial(_ffn_kernel, mod_row=mod_row, final_norm=final_norm)
    return pl.pallas_call(
        kern,
        grid=(n_rows // tm,),
        in_specs=[
            pl.BlockSpec((tm, D_MODEL), lambda i: (i, 0)),
            pl.BlockSpec((1, N_MOD, D_MODEL), lambda i: (i // steps_per_seq, 0, 0)),
            _resident((1, D_MODEL)),
            _resident((D_MODEL, D_FF)),
            _resident((D_MODEL, D_FF)),
            _resident((D_FF, D_MODEL)),
            _resident((1, D_MODEL)),
        ],
        out_specs=pl.BlockSpec((tm, D_MODEL), lambda i: (i, 0)),
        out_shape=jax.ShapeDtypeStruct((n_rows, D_MODEL), F32),
        compiler_params=pltpu.CompilerParams(
            dimension_semantics=("arbitrary",), vmem_limit_bytes=VMEM_LIMIT_BYTES),
        name="ffn_final" if final_norm else "ffn",
    )(x2d, mod, gain, wg, wu, wd, fgain)


def _inverse_masks(c):
    row = lax.broadcasted_iota(jnp.int32, (c, c), 0)
    col = lax.broadcasted_iota(jnp.int32, (c, c), 1)
    base_shift = INV_BASE.bit_length() - 1
    eye = (row == col).astype(F32)
    same_base = (row >> base_shift) == (col >> base_shift)
    offs = []
    sh = base_shift
    while (1 << sh) < c:
        offs.append(((row >> (sh + 1)) == (col >> (sh + 1))) & ((row >> sh) != (col >> sh)))
        sh += 1
    return eye, same_base, offs


def _tri_inverse_many(ms, masks):
    c = ms[0].shape[0]
    eye, same_base, offs = masks
    ns = [jnp.where(same_base, -m, 0.0) for m in ms]
    xs = [eye + n for n in ns]
    nbs = [n.astype(BF16) for n in ns]
    ps = [_dot(nb, nb) for nb in nbs]
    size = 2
    while size < INV_BASE:
        pbs = [p.astype(BF16) for p in ps]
        if 2 * size < INV_BASE:
            xps = [_dot(jnp.concatenate([x, p], axis=0).astype(BF16), pb)
                   for x, p, pb in zip(xs, ps, pbs)]
            xs = [x + xp[:c] for x, xp in zip(xs, xps)]
            ps = [xp[c:] for xp in xps]
        else:
            xs = [x + _dot(x.astype(BF16), pb) for x, pb in zip(xs, pbs)]
        size *= 2
    for off in offs:
        xbs = [x.astype(BF16) for x in xs]
        ys = [_dot(jnp.where(off, m, 0.0).astype(BF16), xb).astype(BF16) for m, xb in zip(ms, xbs)]
        xs = [x - _dot(xb, y) for x, xb, y in zip(xs, xbs, ys)]
    return xs


def _mixer_kernel(x_ref, mod_ref, gain_ref, wqkvz_ref, wba_ref, wp_ref, convw_ref, hp_ref,
                  gnorm_ref, poolw_ref, pscale_ref, wout_ref, o_ref,
                  s_ref, convh_ref, poolh_ref, cat_ref):
    tm = x_ref.shape[1]
    t_blk = pl.program_id(1)

    @pl.when(t_blk == 0)
    def _():
        s_ref[...] = jnp.zeros_like(s_ref)
        convh_ref[...] = jnp.zeros_like(convh_ref)
        poolh_ref[...] = jnp.zeros_like(poolh_ref)

    x = x_ref[0]
    shift = mod_ref[0, 3:4, :]
    scale = mod_ref[0, 4:5, :]
    gate = mod_ref[0, 5:6, :]
    h = _norm_mod(x, gain_ref[...], shift, scale).astype(BF16)

    qkvz = _dot(h, wqkvz_ref[...])
    ba = _dot(h, wba_ref[...])
    p = _dot(h, wp_ref[...])

    qkv = qkvz[:, :3 * GDN_WIDTH]
    z = qkvz[:, 3 * GDN_WIDTH:]
    halo = convh_ref[...]
    convh_ref[...] = qkv[tm - CONV_HALO:, :]
    row8 = lax.broadcasted_iota(jnp.int32, (CONV_HALO, 1), 0)
    acc = qkv * convw_ref[CONV_K - 1:CONV_K, :]
    for s in range(1, CONV_K):
        r = pltpu.roll(qkv, s, axis=0)
        hr = pltpu.roll(halo, s, axis=0)
        top = jnp.where(row8 < s, hr, r[:CONV_HALO])
        r = jnp.concatenate([top, r[CONV_HALO:]], axis=0)
        acc = acc + r * convw_ref[CONV_K - 1 - s:CONV_K - s, :]
    qkv = _silu(acc)

    beta_all = jax.nn.sigmoid(ba)
    a_shift = ba + hp_ref[1:2, :]
    softplus = jnp.maximum(a_shift, 0.0) + jnp.log1p(jnp.exp(-jnp.abs(a_shift)))
    g_all = -jnp.exp(hp_ref[0:1, :]) * softplus
    rows = lax.broadcasted_iota(jnp.int32, (tm, 1), 0)
    row_in_chunk = rows & (GDN_CHUNK - 1)
    gc = g_all
    step = 1
    while step < GDN_CHUNK:
        gc = gc + jnp.where(row_in_chunk >= step, pltpu.roll(gc, step, axis=0), 0.0)
        step *= 2
    gc_t = gc.T

    ri = lax.broadcasted_iota(jnp.int32, (GDN_CHUNK, GDN_CHUNK), 0)
    ci = lax.broadcasted_iota(jnp.int32, (GDN_CHUNK, GDN_CHUNK), 1)
    causal = ri >= ci
    strict = ri > ci
    gnorm = gnorm_ref[...]
    inv_masks = _inverse_masks(GDN_CHUNK)

    n_chunks = tm // GDN_CHUNK
    heads = range(GDN_HEADS)
    q_hs, k_hs, v_hs = [], [], []
    for hd in heads:
        lo, hi = hd * HEAD_DIM, (hd + 1) * HEAD_DIM
        q_h = qkv[:, lo:hi]
        k_h = qkv[:, GDN_WIDTH + lo:GDN_WIDTH + hi]
        q_hs.append(q_h * (lax.rsqrt(jnp.sum(q_h * q_h, axis=-1, keepdims=True) + EPS)
                           * HEAD_DIM ** -0.5))
        k_hs.append(k_h * lax.rsqrt(jnp.sum(k_h * k_h, axis=-1, keepdims=True) + EPS))
        v_hs.append(qkv[:, 2 * GDN_WIDTH + lo:2 * GDN_WIDTH + hi])

    probs = [(c, hd) for c in range(n_chunks) for hd in heads]
    decays, e_gs, e_ks, e_lasts, kbs, kqs = [], [], [], [], [], []
    for c, hd in probs:
        r0, r1 = c * GDN_CHUNK, (c + 1) * GDN_CHUNK
        g_col = gc[r0:r1, GDN_HEADS + hd:GDN_HEADS + hd + 1]
        g_row = gc_t[GDN_HEADS + hd:GDN_HEADS + hd + 1, r0:r1]
        g_last = g_col[GDN_CHUNK - 1:GDN_CHUNK, :]
        decays.append(jnp.exp(jnp.where(causal, g_col - g_row, NEG_BIG)))
        e_gs.append(jnp.exp(g_col))
        e_ks.append(jnp.exp(g_last - g_col))
        e_lasts.append(jnp.exp(g_last))
        k_c = k_hs[hd][r0:r1]
        kb = k_c * beta_all[r0:r1, hd:hd + 1]
        kbs.append(kb)
        kqs.append(_dot_nt(jnp.concatenate([kb, q_hs[hd][r0:r1]], axis=0).astype(BF16),
                           k_c.astype(BF16)))
    ms = [jnp.where(strict, kq[:GDN_CHUNK] * d, 0.0) for kq, d in zip(kqs, decays)]
    intras = [(kq[GDN_CHUNK:] * d).astype(BF16) for kq, d in zip(kqs, decays)]
    t_invs = _tri_inverse_many(ms, inv_masks)
    uws = []
    for i, (c, hd) in enumerate(probs):
        r0, r1 = c * GDN_CHUNK, (c + 1) * GDN_CHUNK
        vb = v_hs[hd][r0:r1] * beta_all[r0:r1, hd:hd + 1]
        uws.append(_dot(t_invs[i].astype(BF16),
                        jnp.concatenate([vb, kbs[i] * e_gs[i]], axis=1).astype(BF16)))

    states = [s_ref[hd] for hd in heads]
    for c in range(n_chunks):
        r0, r1 = c * GDN_CHUNK, (c + 1) * GDN_CHUNK
        idx = [c * GDN_HEADS + hd for hd in heads]
        s_bfs = [st.astype(BF16) for st in states]
        ws = [_dot(uws[i][:, HEAD_DIM:].astype(BF16), s_bfs[hd]) for hd, i in zip(heads, idx)]
        v_news = [(uws[i][:, :HEAD_DIM] - w).astype(BF16) for i, w in zip(idx, ws)]
        new_states = []
        for hd, i in zip(heads, idx):
            k_dec = (k_hs[hd][r0:r1] * e_ks[i]).astype(BF16)
            new_states.append(states[hd] * e_lasts[i] + _dot_tn(k_dec, v_news[hd]))
        for hd, i in zip(heads, idx):
            lo, hi = hd * HEAD_DIM, (hd + 1) * HEAD_DIM
            q_dec = q_hs[hd][r0:r1] * e_gs[i]
            o = _dot(jnp.concatenate([q_dec.astype(BF16), intras[i]], axis=1),
                     jnp.concatenate([s_bfs[hd], v_news[hd]], axis=0))
            o = o * lax.rsqrt(jnp.mean(o * o, axis=-1, keepdims=True) + EPS) * gnorm
            o = o * _silu(z[r0:r1, lo:hi])
            cat_ref[r0:r1, lo:hi] = o.astype(BF16)
        states = new_states
    for hd in heads:
        s_ref[hd] = states[hd]

    ext = jnp.concatenate([poolh_ref[...], p], axis=0)
    poolh_ref[...] = p[tm - POOL_HALO:, :]
    t_glob = (t_blk * tm + rows + 1).astype(F32)
    for gi, win in enumerate(POOL_WINDOWS):
        lo, hi = gi * POOL_GROUP_DIM, (gi + 1) * POOL_GROUP_DIM
        e = ext[:, lo:hi]
        ws = e
        sh = 1
        while sh < win:
            ws = ws + pltpu.roll(ws, sh, axis=0)
            sh *= 2
        cnt = jnp.minimum(t_glob, float(win))
        pooled = ws[POOL_HALO:] / cnt - p[:, lo:hi]
        po = _dot(pooled.astype(BF16), poolw_ref[gi]) * pscale_ref[:, lo:hi]
        cat_ref[:, GDN_WIDTH + lo:GDN_WIDTH + hi] = po.astype(BF16)

    mixed = _dot(cat_ref[...], wout_ref[...])
    o_ref[0] = x + gate * mixed


def _mixer_call(x, mod, gain, wqkvz, wba, wp, convw, hp, gnorm, poolw, pscale, wout):
    B, T, _ = x.shape
    tm = MIX_ROWS
    return pl.pallas_call(
        _mixer_kernel,
        grid=(B, T // tm),
        in_specs=[
            pl.BlockSpec((1, tm, D_MODEL), lambda b, t: (b, t, 0)),
            pl.BlockSpec((1, N_MOD, D_MODEL), lambda b, t: (b, 0, 0)),
            _resident((1, D_MODEL)),
            _resident(wqkvz.shape),
            _resident(wba.shape),
            _resident(wp.shape),
            _resident(convw.shape),
            _resident(hp.shape),
            _resident(gnorm.shape),
            _resident(poolw.shape),
            _resident(pscale.shape),
            _resident(wout.shape),
        ],
        out_specs=pl.BlockSpec((1, tm, D_MODEL), lambda b, t: (b, t, 0)),
        out_shape=jax.ShapeDtypeStruct((B, T, D_MODEL), F32),
        scratch_shapes=[
            pltpu.VMEM((GDN_HEADS, HEAD_DIM, HEAD_DIM), F32),
            pltpu.VMEM((CONV_HALO, 3 * GDN_WIDTH), F32),
            pltpu.VMEM((POOL_HALO, POOL_WIDTH), F32),
            pltpu.VMEM((tm, D_MODEL), BF16),
        ],
        compiler_params=pltpu.CompilerParams(
            dimension_semantics=("arbitrary", "arbitrary"), vmem_limit_bytes=VMEM_LIMIT_BYTES),
        name="mixer",
    )(x, mod, gain, wqkvz, wba, wp, convw, hp, gnorm, poolw, pscale, wout)


def kernel(x, c, w_ada, b_ada, norm_ffn1, ffn1_gate, ffn1_up, ffn1_down, norm_mix, w_in, conv_w,
           a_log, dt_bias, gdn_norm, pool_w, pool_scale, w_out, norm_ffn2, ffn2_gate, ffn2_up,
           ffn2_down, final_norm):
    B, T, D = x.shape
    depth = w_ada.shape[0]
    H, GW = GDN_HEADS, GDN_WIDTH
    fgain = final_norm.reshape(1, D)
    for l in range(depth):
        mod = _ada_call(c, w_ada[l], b_ada[l]).reshape(B, N_MOD, D)

        last = l == depth - 1
        x2d = _ffn_call(x.reshape(B * T, D), mod, norm_ffn1[l].reshape(1, D),
                        ffn1_gate[l].astype(BF16), ffn1_up[l].astype(BF16),
                        ffn1_down[l].astype(BF16), fgain, mod_row=0, final_norm=False, seq=T)

        wi = w_in[l]
        wqkvz = wi[:, :4 * GW].astype(BF16)
        wba = jnp.pad(wi[:, 4 * GW:4 * GW + 2 * H], ((0, 0), (0, LANES - 2 * H))).astype(BF16)
        wp = wi[:, 4 * GW + 2 * H:].astype(BF16)
        hp = jnp.zeros((2, LANES), F32)
        hp = hp.at[0, H:2 * H].set(a_log[l]).at[1, H:2 * H].set(dt_bias[l])
        x3d = _mixer_call(x2d.reshape(B, T, D), mod, norm_mix[l].reshape(1, D), wqkvz, wba, wp,
                          conv_w[l], hp, gdn_norm[l].reshape(1, HEAD_DIM),
                          pool_w[l].astype(BF16), pool_scale[l].reshape(1, POOL_WIDTH),
                          w_out[l].astype(BF16))

        x2d = _ffn_call(x3d.reshape(B * T, D), mod, norm_ffn2[l].reshape(1, D),
                        ffn2_gate[l].astype(BF16), ffn2_up[l].astype(BF16),
                        ffn2_down[l].astype(BF16), fgain, mod_row=6, final_norm=last, seq=T)
        x = x2d.reshape(B, T, D)
    return x
```

```python
import functools

import jax
import jax.numpy as jnp
from jax import lax
from jax.experimental import pallas as pl
from jax.experimental.pallas import tpu as pltpu

F32 = jnp.float32
BF16 = jnp.bfloat16

D_MODEL = 1024
GDN_HEADS = 4
HEAD_DIM = 128
GDN_WIDTH = GDN_HEADS * HEAD_DIM
POOL_WINDOWS = (2, 4, 8, 16)
POOL_GROUPS = len(POOL_WINDOWS)
POOL_GROUP_DIM = 128
POOL_WIDTH = POOL_GROUPS * POOL_GROUP_DIM
CONV_K = 4
D_FF = 2816
N_MOD = 9
EPS = 1e-6

LANES = 128
SUBLANES = 8
VMEM_LIMIT_BYTES = 56 * 1024 * 1024

GDN_CHUNK = LANES
INV_BASE = 16
CONV_HALO = SUBLANES
POOL_HALO = 16
NEG_BIG = -1e30

FFN_ROWS = 1024
FFN_SUBBLOCKS = 4
MIX_ROWS = 1024
GDN_GROUP = 16


def _dot(a, b):
    return jnp.dot(a, b, preferred_element_type=F32)


def _dot_nt(a, b):
    return lax.dot_general(a, b, (((1,), (1,)), ((), ())), preferred_element_type=F32)


def _dot_tn(a, b):
    return lax.dot_general(a, b, (((0,), (0,)), ((), ())), preferred_element_type=F32)


def _silu(x):
    return x * jax.nn.sigmoid(x)


def _norm_mod(x, gain, shift, scale):
    ms = jnp.mean(x * x, axis=-1, keepdims=True)
    y = x * lax.rsqrt(ms + EPS) * gain
    return y * (1.0 + scale) + shift


def _ada_kernel(c_ref, w_ref, b_ref, o_ref):
    a = _silu(c_ref[...])
    o_ref[...] = jnp.dot(a, w_ref[...], preferred_element_type=F32,
                         precision=lax.Precision.HIGHEST) + b_ref[...]


def _ada_call(c, w_ada, b_ada):
    B = c.shape[0]
    n = w_ada.shape[1]
    bn = D_MODEL
    return pl.pallas_call(
        _ada_kernel,
        grid=(n // bn,),
        in_specs=[
            pl.BlockSpec((B, D_MODEL), lambda j: (0, 0)),
            pl.BlockSpec((D_MODEL, bn), lambda j: (0, j)),
            pl.BlockSpec((1, bn), lambda j: (0, j)),
        ],
        out_specs=pl.BlockSpec((B, bn), lambda j: (0, j)),
        out_shape=jax.ShapeDtypeStruct((B, n), F32),
        compiler_params=pltpu.CompilerParams(dimension_semantics=("arbitrary",)),
        name="adaln_mod",
    )(c, w_ada, b_ada.reshape(1, n))


def _ffn_kernel(x_ref, mod_ref, gain_ref, wg_ref, wu_ref, wd_ref, fgain_ref, o_ref, *,
                mod_row, final_norm):
    x = x_ref[...]
    shift = mod_ref[0, mod_row:mod_row + 1, :]
    scale = mod_ref[0, mod_row + 1:mod_row + 2, :]
    gate = mod_ref[0, mod_row + 2:mod_row + 3, :]
    tm = x.shape[0]
    sub = tm // FFN_SUBBLOCKS
    xs = [x[i * sub:(i + 1) * sub] for i in range(FFN_SUBBLOCKS)]
    hs = [_norm_mod(xi, gain_ref[...], shift, scale).astype(BF16) for xi in xs]
    gus = [(_dot(h, wg_ref[...]), _dot(h, wu_ref[...])) for h in hs]
    acts = [(_silu(g) * u).astype(BF16) for g, u in gus]
    ds = [_dot(a, wd_ref[...]) for a in acts]
    for i, (xi, d) in enumerate(zip(xs, ds)):
        y = xi + (0.5 * gate) * d
        if final_norm:
            ms = jnp.mean(y * y, axis=-1, keepdims=True)
            y = y * lax.rsqrt(ms + EPS) * fgain_ref[...]
        o_ref[i * sub:(i + 1) * sub, :] = y


def _resident(shape):
    nd = len(shape)
    return pl.BlockSpec(shape, lambda *_: (0,) * nd, pipeline_mode=pl.Buffered(1))


def _ffn_call(x2d, mod, gain, wg, wu, wd, fgain, *, mod_row, final_norm, seq):
    n_rows = x2d.shape[0]
    tm = FFN_ROWS
    steps_per_seq = seq // tm
    kern = functools.partial(_ffn_kernel, mod_row=mod_row, final_norm=final_norm)
    return pl.pallas_call(
        kern,
        grid=(n_rows // tm,),
        in_specs=[
            pl.BlockSpec((tm, D_MODEL), lambda i: (i, 0)),
            pl.BlockSpec((1, N_MOD, D_MODEL), lambda i: (i // steps_per_seq, 0, 0)),
            _resident((1, D_MODEL)),
            _resident((D_MODEL, D_FF)),
            _resident((D_MODEL, D_FF)),
            _resident((D_FF, D_MODEL)),
            _resident((1, D_MODEL)),
        ],
        out_specs=pl.BlockSpec((tm, D_MODEL), lambda i: (i, 0)),
        out_shape=jax.ShapeDtypeStruct((n_rows, D_MODEL), F32),
        compiler_params=pltpu.CompilerParams(
            dimension_semantics=("arbitrary",), vmem_limit_bytes=VMEM_LIMIT_BYTES),
        name="ffn_final" if final_norm else "ffn",
    )(x2d, mod, gain, wg, wu, wd, fgain)


def _inverse_masks(c):
    row = lax.broadcasted_iota(jnp.int32, (c, c), 0)
    col = lax.broadcasted_iota(jnp.int32, (c, c), 1)
    base_shift = INV_BASE.bit_length() - 1
    eye = (row == col).astype(F32)
    same_base = (row >> base_shift) == (col >> base_shift)
    offs = []
    sh = base_shift
    while (1 << sh) < c:
        offs.append(((row >> (sh + 1)) == (col >> (sh + 1))) & ((row >> sh) != (col >> sh)))
        sh += 1
    return eye, same_base, offs


def _tri_inverse_many(ms, masks):
    c = ms[0].shape[0]
    eye, same_base, offs = masks
    ns = [jnp.where(same_base, -m, 0.0) for m in ms]
    xs = [eye + n for n in ns]
    nbs = [n.astype(BF16) for n in ns]
    ps = [_dot(nb, nb) for nb in nbs]
    size = 2
    while size < INV_BASE:
        pbs = [p.astype(BF16) for p in ps]
        if 2 * size < INV_BASE:
            xps = [_dot(jnp.concatenate([x, p], axis=0).astype(BF16), pb)
                   for x, p, pb in zip(xs, ps, pbs)]
            xs = [x + xp[:c] for x, xp in zip(xs, xps)]
            ps = [xp[c:] for xp in xps]
        else:
            xs = [x + _dot(x.astype(BF16), pb) for x, pb in zip(xs, pbs)]
        size *= 2
    for off in offs:
        xbs = [x.astype(BF16) for x in xs]
        ys = [_dot(jnp.where(off, m, 0.0).astype(BF16), xb).astype(BF16) for m, xb in zip(ms, xbs)]
        xs = [x - _dot(xb, y) for x, xb, y in zip(xs, xbs, ys)]
    return xs


def _mixer_kernel(x_ref, mod_ref, gain_ref, wqkvz_ref, wba_ref, wp_ref, convw_ref, hp_ref,
                  gnorm_ref, poolw_ref, pscale_ref, wout_ref, o_ref,
                  s_ref, conv_ref, poolh_ref, cat_ref):
    tm = x_ref.shape[1]
    t_blk = pl.program_id(1)

    @pl.when(t_blk == 0)
    def _():
        s_ref[...] = jnp.zeros_like(s_ref)
        conv_ref[0:CONV_HALO, :] = jnp.zeros((CONV_HALO, conv_ref.shape[1]), F32)
        poolh_ref[...] = jnp.zeros_like(poolh_ref)

    @pl.when(t_blk > 0)
    def _():
        conv_ref[0:CONV_HALO, :] = conv_ref[tm:tm + CONV_HALO, :]

    x = x_ref[0]
    shift = mod_ref[0, 3:4, :]
    scale = mod_ref[0, 4:5, :]
    gate = mod_ref[0, 5:6, :]
    h = _norm_mod(x, gain_ref[...], shift, scale).astype(BF16)

    ba = _dot(h, wba_ref[...])
    conv_ref[CONV_HALO:, :] = _dot(h, wqkvz_ref[:, :3 * GDN_WIDTH])
    z = _dot(h, wqkvz_ref[:, 3 * GDN_WIDTH:])
    p = _dot(h, wp_ref[...])

    beta_all = jax.nn.sigmoid(ba)
    a_shift = ba + hp_ref[1:2, :]
    softplus = jnp.maximum(a_shift, 0.0) + jnp.log1p(jnp.exp(-jnp.abs(a_shift)))
    g_all = -jnp.exp(hp_ref[0:1, :]) * softplus
    rows = lax.broadcasted_iota(jnp.int32, (tm, 1), 0)
    row_in_chunk = rows & (GDN_CHUNK - 1)
    gc = g_all
    step = 1
    while step < GDN_CHUNK:
        gc = gc + jnp.where(row_in_chunk >= step, pltpu.roll(gc, step, axis=0), 0.0)
        step *= 2
    e_g_all = jnp.exp(gc)
    gc_t = gc.T[:SUBLANES]

    acc = None
    for s in range(CONV_K):
        term = (conv_ref[CONV_HALO - s:CONV_HALO - s + tm, :]
                * convw_ref[CONV_K - 1 - s:CONV_K - s, :])
        acc = term if acc is None else acc + term
    qkv = _silu(acc)

    ri = lax.broadcasted_iota(jnp.int32, (GDN_CHUNK, GDN_CHUNK), 0)
    ci = lax.broadcasted_iota(jnp.int32, (GDN_CHUNK, GDN_CHUNK), 1)
    causal = ri >= ci
    strict = ri > ci
    gnorm = gnorm_ref[...]
    inv_masks = _inverse_masks(GDN_CHUNK)

    n_chunks = tm // GDN_CHUNK
    heads = range(GDN_HEADS)
    q_hs, k_hs, kt_hs, v_hs = [], [], [], []
    for hd in heads:
        lo, hi = hd * HEAD_DIM, (hd + 1) * HEAD_DIM
        q_h = qkv[:, lo:hi]
        k_h = qkv[:, GDN_WIDTH + lo:GDN_WIDTH + hi]
        q_hs.append(q_h * (lax.rsqrt(jnp.sum(q_h * q_h, axis=-1, keepdims=True) + EPS)
                           * HEAD_DIM ** -0.5))
        k_h = k_h * lax.rsqrt(jnp.sum(k_h * k_h, axis=-1, keepdims=True) + EPS)
        k_hs.append(k_h)
        kt_hs.append(k_h.T)
        v_hs.append(qkv[:, 2 * GDN_WIDTH + lo:2 * GDN_WIDTH + hi])

    probs = [(c, hd) for c in range(n_chunks) for hd in heads]
    e_gs, kt_decs, e_lasts, intras, uws = [], [], [], [], []
    for g0 in range(0, len(probs), GDN_GROUP):
        group = probs[g0:g0 + GDN_GROUP]
        decays, kbs, kqs = [], [], []
        for c, hd in group:
            r0, r1 = c * GDN_CHUNK, (c + 1) * GDN_CHUNK
            lane = GDN_HEADS + hd
            g_col = gc[r0:r1, lane:lane + 1]
            g_row = gc_t[lane:lane + 1, r0:r1]
            g_last = gc_t[lane:lane + 1, r1 - 1:r1]
            decays.append(jnp.exp(jnp.where(causal, g_col - g_row, NEG_BIG)))
            e_gs.append(e_g_all[r0:r1, lane:lane + 1])
            e_lasts.append(jnp.exp(g_last))
            kt_c = kt_hs[hd][:, r0:r1]
            kt_decs.append((kt_c * jnp.exp(g_last - g_row)).astype(BF16))
            kb = k_hs[hd][r0:r1] * beta_all[r0:r1, hd:hd + 1]
            kbs.append(kb)
            kqs.append(_dot(jnp.concatenate([kb, q_hs[hd][r0:r1]], axis=0).astype(BF16),
                            kt_c.astype(BF16)))
        ms = [jnp.where(strict, kq[:GDN_CHUNK] * d, 0.0) for kq, d in zip(kqs, decays)]
        intras += [(kq[GDN_CHUNK:] * d).astype(BF16) for kq, d in zip(kqs, decays)]
        t_invs = _tri_inverse_many(ms, inv_masks)
        for j, (c, hd) in enumerate(group):
            r0, r1 = c * GDN_CHUNK, (c + 1) * GDN_CHUNK
            vb = v_hs[hd][r0:r1] * beta_all[r0:r1, hd:hd + 1]
            uws.append(_dot(t_invs[j].astype(BF16),
                            jnp.concatenate([vb, kbs[j] * e_gs[g0 + j]], axis=1).astype(BF16)))

    t_glob = (t_blk * tm + rows + 1).astype(F32)

    def pool_and_project(c, p_c, p_tail):
        r0, r1 = c * GDN_CHUNK, (c + 1) * GDN_CHUNK
        ext = jnp.concatenate([p_tail, p_c], axis=0)
        for gi, win in enumerate(POOL_WINDOWS):
            lo, hi = gi * POOL_GROUP_DIM, (gi + 1) * POOL_GROUP_DIM
            wsum = ext[:, lo:hi]
            sh = 1
            while sh < win:
                wsum = wsum + pltpu.roll(wsum, sh, axis=0)
                sh *= 2
            cnt = jnp.minimum(t_glob[r0:r1], float(win))
            pooled = wsum[POOL_HALO:] / cnt - p_c[:, lo:hi]
            po = _dot(pooled.astype(BF16), poolw_ref[gi]) * pscale_ref[:, lo:hi]
            cat_ref[r0:r1, GDN_WIDTH + lo:GDN_WIDTH + hi] = po.astype(BF16)
        mixed = _dot(cat_ref[r0:r1, :], wout_ref[...])
        o_ref[0, r0:r1, :] = x[r0:r1] + gate * mixed

    states = [s_ref[hd] for hd in heads]
    p_prev, p_tail = None, poolh_ref[...]
    for c in range(n_chunks):
        r0, r1 = c * GDN_CHUNK, (c + 1) * GDN_CHUNK
        idx = [c * GDN_HEADS + hd for hd in heads]
        s_bfs = [st.astype(BF16) for st in states]
        ws = [_dot(uws[i][:, HEAD_DIM:].astype(BF16), s_bfs[hd]) for hd, i in zip(heads, idx)]
        z_c = z[r0:r1]
        p_c = p[r0:r1]
        v_news = [(uws[i][:, :HEAD_DIM] - w).astype(BF16) for i, w in zip(idx, ws)]
        new_states = [states[hd] * e_lasts[i] + _dot(kt_decs[i], v_news[hd])
                      for hd, i in zip(heads, idx)]
        for hd, i in zip(heads, idx):
            lo, hi = hd * HEAD_DIM, (hd + 1) * HEAD_DIM
            q_dec = q_hs[hd][r0:r1] * e_gs[i]
            o = _dot(jnp.concatenate([q_dec.astype(BF16), intras[i]], axis=1),
                     jnp.concatenate([s_bfs[hd], v_news[hd]], axis=0))
            o = o * lax.rsqrt(jnp.mean(o * o, axis=-1, keepdims=True) + EPS) * gnorm
            o = o * _silu(z_c[:, lo:hi])
            cat_ref[r0:r1, lo:hi] = o.astype(BF16)
        if c > 0:
            pool_and_project(c - 1, p_prev, p_tail)
            p_tail = p_prev[GDN_CHUNK - POOL_HALO:]
        p_prev = p_c
        states = new_states
    pool_and_project(n_chunks - 1, p_prev, p_tail)
    poolh_ref[...] = p_prev[GDN_CHUNK - POOL_HALO:]
    for hd in heads:
        s_ref[hd] = states[hd]


def _mixer_call(x, mod, gain, wqkvz, wba, wp, convw, hp, gnorm, poolw, pscale, wout):
    B, T, _ = x.shape
    tm = MIX_ROWS
    return pl.pallas_call(
        _mixer_kernel,
        grid=(B, T // tm),
        in_specs=[
            pl.BlockSpec((1, tm, D_MODEL), lambda b, t: (b, t, 0)),
            pl.BlockSpec((1, N_MOD, D_MODEL), lambda b, t: (b, 0, 0)),
            _resident((1, D_MODEL)),
            _resident(wqkvz.shape),
            _resident(wba.shape),
            _resident(wp.shape),
            _resident(convw.shape),
            _resident(hp.shape),
            _resident(gnorm.shape),
            _resident(poolw.shape),
            _resident(pscale.shape),
            _resident(wout.shape),
        ],
        out_specs=pl.BlockSpec((1, tm, D_MODEL), lambda b, t: (b, t, 0)),
        out_shape=jax.ShapeDtypeStruct((B, T, D_MODEL), F32),
        scratch_shapes=[
            pltpu.VMEM((GDN_HEADS, HEAD_DIM, HEAD_DIM), F32),
            pltpu.VMEM((CONV_HALO + tm, 3 * GDN_WIDTH), F32),
            pltpu.VMEM((POOL_HALO, POOL_WIDTH), F32),
            pltpu.VMEM((tm, D_MODEL), BF16),
        ],
        compiler_params=pltpu.CompilerParams(
            dimension_semantics=("arbitrary", "arbitrary"), vmem_limit_bytes=VMEM_LIMIT_BYTES),
        name="mixer",
    )(x, mod, gain, wqkvz, wba, wp, convw, hp, gnorm, poolw, pscale, wout)


def kernel(x, c, w_ada, b_ada, norm_ffn1, ffn1_gate, ffn1_up, ffn1_down, norm_mix, w_in, conv_w,
           a_log, dt_bias, gdn_norm, pool_w, pool_scale, w_out, norm_ffn2, ffn2_gate, ffn2_up,
           ffn2_down, final_norm):
    B, T, D = x.shape
    depth = w_ada.shape[0]
    H, GW = GDN_HEADS, GDN_WIDTH
    fgain = final_norm.reshape(1, D)
    for l in range(depth):
        mod = _ada_call(c, w_ada[l], b_ada[l]).reshape(B, N_MOD, D)

        last = l == depth - 1
        x2d = _ffn_call(x.reshape(B * T, D), mod, norm_ffn1[l].reshape(1, D),
                        ffn1_gate[l].astype(BF16), ffn1_up[l].astype(BF16),
                        ffn1_down[l].astype(BF16), fgain, mod_row=0, final_norm=False, seq=T)

        wi = w_in[l]
        wqkvz = wi[:, :4 * GW].astype(BF16)
        wba = jnp.pad(wi[:, 4 * GW:4 * GW + 2 * H], ((0, 0), (0, LANES - 2 * H))).astype(BF16)
        wp = wi[:, 4 * GW + 2 * H:].astype(BF16)
        hp = jnp.zeros((2, LANES), F32)
        hp = hp.at[0, H:2 * H].set(a_log[l]).at[1, H:2 * H].set(dt_bias[l])
        x3d = _mixer_call(x2d.reshape(B, T, D), mod, norm_mix[l].reshape(1, D), wqkvz, wba, wp,
                          conv_w[l], hp, gdn_norm[l].reshape(1, HEAD_DIM),
                          pool_w[l].astype(BF16), pool_scale[l].reshape(1, POOL_WIDTH),
                          w_out[l].astype(BF16))

        x2d = _ffn_call(x3d.reshape(B * T, D), mod, norm_ffn2[l].reshape(1, D),
                        ffn2_gate[l].astype(BF16), ffn2_up[l].astype(BF16),
                        ffn2_down[l].astype(BF16), fgain, mod_row=6, final_norm=last, seq=T)
        x = x2d.reshape(B, T, D)
    return x
```

```python
import functools

import jax
import jax.numpy as jnp
from jax import lax
from jax.experimental import pallas as pl
from jax.experimental.pallas import tpu as pltpu

F32 = jnp.float32
BF16 = jnp.bfloat16

D_MODEL = 1024
GDN_HEADS = 4
HEAD_DIM = 128
GDN_WIDTH = GDN_HEADS * HEAD_DIM
POOL_WINDOWS = (2, 4, 8, 16)
POOL_GROUPS = len(POOL_WINDOWS)
POOL_GROUP_DIM = 128
POOL_WIDTH = POOL_GROUPS * POOL_GROUP_DIM
CONV_K = 4
D_FF = 2816
N_MOD = 9
EPS = 1e-6

LANES = 128
SUBLANES = 8
VMEM_LIMIT_BYTES = 56 * 1024 * 1024

GDN_CHUNK = LANES
INV_BASE = 16
CONV_HALO = SUBLANES
POOL_HALO = 16
NEG_BIG = -1e30

FFN_ROWS = 1024
FFN_SUBBLOCKS = 4
MIX_ROWS = 1024
GDN_GROUP = 32


def _dot(a, b):
    return jnp.dot(a, b, preferred_element_type=F32)


def _silu(x):
    return x * jax.nn.sigmoid(x)


def _norm_mod(x, gain, shift, scale):
    ms = jnp.mean(x * x, axis=-1, keepdims=True)
    y = x * lax.rsqrt(ms + EPS) * gain
    return y * (1.0 + scale) + shift


def _ada_kernel(c_ref, w_ref, b_ref, o_ref):
    a = _silu(c_ref[...])
    o_ref[...] = jnp.dot(a, w_ref[...], preferred_element_type=F32,
                         precision=lax.Precision.HIGHEST) + b_ref[...]


def _ada_call(c, w_ada, b_ada):
    B = c.shape[0]
    n = w_ada.shape[1]
    bn = D_MODEL
    return pl.pallas_call(
        _ada_kernel,
        grid=(n // bn,),
        in_specs=[
            pl.BlockSpec((B, D_MODEL), lambda j: (0, 0)),
            pl.BlockSpec((D_MODEL, bn), lambda j: (0, j)),
            pl.BlockSpec((1, bn), lambda j: (0, j)),
        ],
        out_specs=pl.BlockSpec((B, bn), lambda j: (0, j)),
        out_shape=jax.ShapeDtypeStruct((B, n), F32),
        compiler_params=pltpu.CompilerParams(dimension_semantics=("arbitrary",)),
        name="adaln_mod",
    )(c, w_ada, b_ada.reshape(1, n))


def _ffn_kernel(x_ref, mod_ref, gain_ref, wg_ref, wu_ref, wd_ref, fgain_ref, o_ref, *,
                mod_row, final_norm):
    x = x_ref[...]
    shift = mod_ref[0, mod_row:mod_row + 1, :]
    scale = mod_ref[0, mod_row + 1:mod_row + 2, :]
    gate = mod_ref[0, mod_row + 2:mod_row + 3, :]
    tm = x.shape[0]
    sub = tm // FFN_SUBBLOCKS
    xs = [x[i * sub:(i + 1) * sub] for i in range(FFN_SUBBLOCKS)]
    hs = [_norm_mod(xi, gain_ref[...], shift, scale).astype(BF16) for xi in xs]
    gus = [(_dot(h, wg_ref[...]), _dot(h, wu_ref[...])) for h in hs]
    acts = [(_silu(g) * u).astype(BF16) for g, u in gus]
    ds = [_dot(a, wd_ref[...]) for a in acts]
    for i, (xi, d) in enumerate(zip(xs, ds)):
        y = xi + (0.5 * gate) * d
        if final_norm:
            ms = jnp.mean(y * y, axis=-1, keepdims=True)
            y = y * lax.rsqrt(ms + EPS) * fgain_ref[...]
        o_ref[i * sub:(i + 1) * sub, :] = y


def _resident(shape):
    nd = len(shape)
    return pl.BlockSpec(shape, lambda *_: (0,) * nd, pipeline_mode=pl.Buffered(1))


def _ffn_call(x2d, mod, gain, wg, wu, wd, fgain, *, mod_row, final_norm, seq):
    n_rows = x2d.shape[0]
    tm = FFN_ROWS
    steps_per_seq = seq // tm
    kern = functools.partial(_ffn_kernel, mod_row=mod_row, final_norm=final_norm)
    return pl.pallas_call(
        kern,
        grid=(n_rows // tm,),
        in_specs=[
            pl.BlockSpec((tm, D_MODEL), lambda i: (i, 0)),
            pl.BlockSpec((1, N_MOD, D_MODEL), lambda i: (i // steps_per_seq, 0, 0)),
            _resident((1, D_MODEL)),
            _resident((D_MODEL, D_FF)),
            _resident((D_MODEL, D_FF)),
            _resident((D_FF, D_MODEL)),
            _resident((1, D_MODEL)),
        ],
        out_specs=pl.BlockSpec((tm, D_MODEL), lambda i: (i, 0)),
        out_shape=jax.ShapeDtypeStruct((n_rows, D_MODEL), F32),
        compiler_params=pltpu.CompilerParams(
            dimension_semantics=("arbitrary",), vmem_limit_bytes=VMEM_LIMIT_BYTES),
        name="ffn_final" if final_norm else "ffn",
    )(x2d, mod, gain, wg, wu, wd, fgain)


def _inverse_masks(c):
    row = lax.broadcasted_iota(jnp.int32, (c, c), 0)
    col = lax.broadcasted_iota(jnp.int32, (c, c), 1)
    base_shift = INV_BASE.bit_length() - 1
    eye = (row == col).astype(F32)
    same_base = (row >> base_shift) == (col >> base_shift)
    offs = []
    sh = base_shift
    while (1 << sh) < c:
        offs.append(((row >> (sh + 1)) == (col >> (sh + 1))) & ((row >> sh) != (col >> sh)))
        sh += 1
    return eye, same_base, offs


def _tri_inverse_many(ms, masks):
    c = ms[0].shape[0]
    eye, same_base, offs = masks
    ds = [jnp.where(same_base, m, 0.0) for m in ms]
    xs = [eye - d for d in ds]
    dbs = [d.astype(BF16) for d in ds]
    ps = [_dot(db, db) for db in dbs]
    size = 2
    while size < INV_BASE:
        pbs = [p.astype(BF16) for p in ps]
        if 2 * size < INV_BASE:
            xps = [_dot(jnp.concatenate([x.astype(BF16), pb], axis=0), pb)
                   for x, pb in zip(xs, pbs)]
            xs = [x + xp[:c] for x, xp in zip(xs, xps)]
            ps = [xp[c:] for xp in xps]
        else:
            xs = [x + _dot(x.astype(BF16), pb) for x, pb in zip(xs, pbs)]
        size *= 2
    for off in offs:
        xbs = [x.astype(BF16) for x in xs]
        ys = [_dot(jnp.where(off, m, 0.0).astype(BF16), xb).astype(BF16) for m, xb in zip(ms, xbs)]
        xs = [x - _dot(xb, y) for x, xb, y in zip(xs, xbs, ys)]
    return xs


def _mixer_kernel(x_ref, mod_ref, gain_ref, wqkvz_ref, wba_ref, wp_ref, convw_ref, hp_ref,
                  gnorm_ref, poolw_ref, pscale_ref, wout_ref, o_ref,
                  s_ref, conv_ref, poolh_ref, wcomb_ref):
    tm = x_ref.shape[1]
    t_blk = pl.program_id(1)

    @pl.when(t_blk == 0)
    def _():
        s_ref[...] = jnp.zeros_like(s_ref)
        conv_ref[0:CONV_HALO, :] = jnp.zeros((CONV_HALO, conv_ref.shape[1]), F32)
        poolh_ref[...] = jnp.zeros_like(poolh_ref)

    @pl.when(t_blk > 0)
    def _():
        conv_ref[0:CONV_HALO, :] = conv_ref[tm:tm + CONV_HALO, :]

    @pl.when((pl.program_id(0) == 0) & (t_blk == 0))
    def _():
        wcomb_ref[0:GDN_WIDTH, :] = wout_ref[0:GDN_WIDTH, :]
        for gi in range(POOL_GROUPS):
            lo, hi = gi * POOL_GROUP_DIM, (gi + 1) * POOL_GROUP_DIM
            scaled = (poolw_ref[gi] * pscale_ref[:, lo:hi]).astype(BF16)
            wcomb_ref[GDN_WIDTH + lo:GDN_WIDTH + hi, :] = _dot(
                scaled, wout_ref[GDN_WIDTH + lo:GDN_WIDTH + hi, :]).astype(BF16)

    x = x_ref[0]
    shift = mod_ref[0, 3:4, :]
    scale = mod_ref[0, 4:5, :]
    gate = mod_ref[0, 5:6, :]
    h = _norm_mod(x, gain_ref[...], shift, scale).astype(BF16)

    ba = _dot(h, wba_ref[...])
    conv_ref[CONV_HALO:, :] = _dot(h, wqkvz_ref[:, :3 * GDN_WIDTH])
    z = _dot(h, wqkvz_ref[:, 3 * GDN_WIDTH:])
    p = _dot(h, wp_ref[...])

    beta_all = jax.nn.sigmoid(ba)
    a_shift = ba + hp_ref[1:2, :]
    softplus = jnp.maximum(a_shift, 0.0) + jnp.log1p(jnp.exp(-jnp.abs(a_shift)))
    g_all = -jnp.exp(hp_ref[0:1, :]) * softplus
    rows = lax.broadcasted_iota(jnp.int32, (tm, 1), 0)
    row_in_chunk = rows & (GDN_CHUNK - 1)
    gc = g_all
    step = 1
    while step < GDN_CHUNK:
        gc = gc + jnp.where(row_in_chunk >= step, pltpu.roll(gc, step, axis=0), 0.0)
        step *= 2
    e_g_all = jnp.exp(gc)
    gc_t = gc.T[:SUBLANES]

    n_slabs = (CONV_HALO + tm) // SUBLANES
    slabs = conv_ref[...].reshape(n_slabs, SUBLANES, 3 * GDN_WIDTH)
    first_row = lax.broadcasted_iota(jnp.int32, (1, SUBLANES, 1), 1) == 0
    acc = slabs * convw_ref[0:1, :].reshape(1, 1, 3 * GDN_WIDTH)
    for j in range(1, CONV_K):
        rot = pltpu.roll(acc, 1, axis=1)
        above = jnp.concatenate([rot[:1], rot[:-1]], axis=0)
        acc = (jnp.where(first_row, above, rot)
               + slabs * convw_ref[j:j + 1, :].reshape(1, 1, 3 * GDN_WIDTH))
    qkv = _silu(acc[1:]).reshape(tm, 3 * GDN_WIDTH)

    ri = lax.broadcasted_iota(jnp.int32, (GDN_CHUNK, GDN_CHUNK), 0)
    ci = lax.broadcasted_iota(jnp.int32, (GDN_CHUNK, GDN_CHUNK), 1)
    causal = ri >= ci
    strict = ri > ci
    gnorm = gnorm_ref[...]
    inv_masks = _inverse_masks(GDN_CHUNK)

    n_chunks = tm // GDN_CHUNK
    heads = range(GDN_HEADS)
    q_hs, k_hs, kt_hs, v_hs = [], [], [], []
    for hd in heads:
        lo, hi = hd * HEAD_DIM, (hd + 1) * HEAD_DIM
        q_h = qkv[:, lo:hi]
        k_h = qkv[:, GDN_WIDTH + lo:GDN_WIDTH + hi]
        q_hs.append(q_h * (lax.rsqrt(jnp.sum(q_h * q_h, axis=-1, keepdims=True) + EPS)
                           * HEAD_DIM ** -0.5))
        k_h = k_h * lax.rsqrt(jnp.sum(k_h * k_h, axis=-1, keepdims=True) + EPS)
        k_hs.append(k_h)
        kt_hs.append(k_h.T)
        v_hs.append(qkv[:, 2 * GDN_WIDTH + lo:2 * GDN_WIDTH + hi])

    probs = [(c, hd) for c in range(n_chunks) for hd in heads]
    e_gs, kt_decs, e_lasts, intras, uws = [], [], [], [], []
    for g0 in range(0, len(probs), GDN_GROUP):
        group = probs[g0:g0 + GDN_GROUP]
        decays, kbs, kqs = [], [], []
        for c, hd in group:
            r0, r1 = c * GDN_CHUNK, (c + 1) * GDN_CHUNK
            lane = GDN_HEADS + hd
            g_col = gc[r0:r1, lane:lane + 1]
            g_row = gc_t[lane:lane + 1, r0:r1]
            g_last = gc_t[lane:lane + 1, r1 - 1:r1]
            decays.append(jnp.exp(jnp.where(causal, g_col - g_row, NEG_BIG)))
            e_gs.append(e_g_all[r0:r1, lane:lane + 1])
            e_lasts.append(jnp.exp(g_last))
            kt_c = kt_hs[hd][:, r0:r1]
            kt_decs.append((kt_c * jnp.exp(g_last - g_row)).astype(BF16))
            kb = k_hs[hd][r0:r1] * beta_all[r0:r1, hd:hd + 1]
            kbs.append(kb)
            kqs.append(_dot(jnp.concatenate([kb, q_hs[hd][r0:r1]], axis=0).astype(BF16),
                            kt_c.astype(BF16)))
        ms = [jnp.where(strict, kq[:GDN_CHUNK] * d, 0.0) for kq, d in zip(kqs, decays)]
        intras += [(kq[GDN_CHUNK:] * d).astype(BF16) for kq, d in zip(kqs, decays)]
        t_invs = _tri_inverse_many(ms, inv_masks)
        for j, (c, hd) in enumerate(group):
            r0, r1 = c * GDN_CHUNK, (c + 1) * GDN_CHUNK
            vb = v_hs[hd][r0:r1] * beta_all[r0:r1, hd:hd + 1]
            uws.append(_dot(t_invs[j].astype(BF16),
                            jnp.concatenate([vb, kbs[j] * e_gs[g0 + j]], axis=1).astype(BF16)))

    t_glob = (t_blk * tm + rows + 1).astype(F32)

    def pool_and_project(c, gdn_outs, p_c, p_tail):
        r0, r1 = c * GDN_CHUNK, (c + 1) * GDN_CHUNK
        ext = jnp.concatenate([p_tail, p_c], axis=0)
        pool_outs = []
        for gi, win in enumerate(POOL_WINDOWS):
            lo, hi = gi * POOL_GROUP_DIM, (gi + 1) * POOL_GROUP_DIM
            wsum = ext[:, lo:hi]
            sh = 1
            while sh < win:
                wsum = wsum + pltpu.roll(wsum, sh, axis=0)
                sh *= 2
            cnt = jnp.minimum(t_glob[r0:r1], float(win))
            pool_outs.append(wsum[POOL_HALO:] / cnt - p_c[:, lo:hi])
        cat = jnp.concatenate(gdn_outs + pool_outs, axis=1).astype(BF16)
        mixed = _dot(cat, wcomb_ref[...])
        o_ref[0, r0:r1, :] = x[r0:r1] + gate * mixed

    states = [s_ref[hd] for hd in heads]
    p_prev, o_prev, p_tail = None, None, poolh_ref[...]
    for c in range(n_chunks):
        r0, r1 = c * GDN_CHUNK, (c + 1) * GDN_CHUNK
        idx = [c * GDN_HEADS + hd for hd in heads]
        s_bfs = [st.astype(BF16) for st in states]
        ws = [_dot(uws[i][:, HEAD_DIM:].astype(BF16), s_bfs[hd]) for hd, i in zip(heads, idx)]
        z_c = z[r0:r1]
        p_c = p[r0:r1]
        v_news = [(uws[i][:, :HEAD_DIM] - w).astype(BF16) for i, w in zip(idx, ws)]
        new_states = [states[hd] * e_lasts[i] + _dot(kt_decs[i], v_news[hd])
                      for hd, i in zip(heads, idx)]
        o_c = []
        for hd, i in zip(heads, idx):
            lo, hi = hd * HEAD_DIM, (hd + 1) * HEAD_DIM
            q_dec = q_hs[hd][r0:r1] * e_gs[i]
            o = _dot(jnp.concatenate([q_dec.astype(BF16), intras[i]], axis=1),
                     jnp.concatenate([s_bfs[hd], v_news[hd]], axis=0))
            o = o * lax.rsqrt(jnp.mean(o * o, axis=-1, keepdims=True) + EPS) * gnorm
            o_c.append(o * _silu(z_c[:, lo:hi]))
        if c > 0:
            pool_and_project(c - 1, o_prev, p_prev, p_tail)
            p_tail = p_prev[GDN_CHUNK - POOL_HALO:]
        p_prev, o_prev = p_c, o_c
        states = new_states
    pool_and_project(n_chunks - 1, o_prev, p_prev, p_tail)
    poolh_ref[...] = p_prev[GDN_CHUNK - POOL_HALO:]
    for hd in heads:
        s_ref[hd] = states[hd]


def _mixer_call(x, mod, gain, wqkvz, wba, wp, convw, hp, gnorm, poolw, pscale, wout):
    B, T, _ = x.shape
    tm = MIX_ROWS
    return pl.pallas_call(
        _mixer_kernel,
        grid=(B, T // tm),
        in_specs=[
            pl.BlockSpec((1, tm, D_MODEL), lambda b, t: (b, t, 0)),
            pl.BlockSpec((1, N_MOD, D_MODEL), lambda b, t: (b, 0, 0)),
            _resident((1, D_MODEL)),
            _resident(wqkvz.shape),
            _resident(wba.shape),
            _resident(wp.shape),
            _resident(convw.shape),
            _resident(hp.shape),
            _resident(gnorm.shape),
            _resident(poolw.shape),
            _resident(pscale.shape),
            _resident(wout.shape),
        ],
        out_specs=pl.BlockSpec((1, tm, D_MODEL), lambda b, t: (b, t, 0)),
        out_shape=jax.ShapeDtypeStruct((B, T, D_MODEL), F32),
        scratch_shapes=[
            pltpu.VMEM((GDN_HEADS, HEAD_DIM, HEAD_DIM), F32),
            pltpu.VMEM((CONV_HALO + tm, 3 * GDN_WIDTH), F32),
            pltpu.VMEM((POOL_HALO, POOL_WIDTH), F32),
            pltpu.VMEM((D_MODEL, D_MODEL), BF16),
        ],
        compiler_params=pltpu.CompilerParams(
            dimension_semantics=("arbitrary", "arbitrary"), vmem_limit_bytes=VMEM_LIMIT_BYTES),
        name="mixer",
    )(x, mod, gain, wqkvz, wba, wp, convw, hp, gnorm, poolw, pscale, wout)


def kernel(x, c, w_ada, b_ada, norm_ffn1, ffn1_gate, ffn1_up, ffn1_down, norm_mix, w_in, conv_w,
           a_log, dt_bias, gdn_norm, pool_w, pool_scale, w_out, norm_ffn2, ffn2_gate, ffn2_up,
           ffn2_down, final_norm):
    B, T, D = x.shape
    depth = w_ada.shape[0]
    H, GW = GDN_HEADS, GDN_WIDTH
    fgain = final_norm.reshape(1, D)
    for l in range(depth):
        mod = _ada_call(c, w_ada[l], b_ada[l]).reshape(B, N_MOD, D)

        last = l == depth - 1
        x2d = _ffn_call(x.reshape(B * T, D), mod, norm_ffn1[l].reshape(1, D),
                        ffn1_gate[l].astype(BF16), ffn1_up[l].astype(BF16),
                        ffn1_down[l].astype(BF16), fgain, mod_row=0, final_norm=False, seq=T)

        wi = w_in[l]
        wqkvz = wi[:, :4 * GW].astype(BF16)
        wba = jnp.pad(wi[:, 4 * GW:4 * GW + 2 * H], ((0, 0), (0, LANES - 2 * H))).astype(BF16)
        wp = wi[:, 4 * GW + 2 * H:].astype(BF16)
        hp = jnp.zeros((2, LANES), F32)
        hp = hp.at[0, H:2 * H].set(a_log[l]).at[1, H:2 * H].set(dt_bias[l])
        x3d = _mixer_call(x2d.reshape(B, T, D), mod, norm_mix[l].reshape(1, D), wqkvz, wba, wp,
                          conv_w[l], hp, gdn_norm[l].reshape(1, HEAD_DIM),
                          pool_w[l], pool_scale[l].reshape(1, POOL_WIDTH),
                          w_out[l].astype(BF16))

        x2d = _ffn_call(x3d.reshape(B * T, D), mod, norm_ffn2[l].reshape(1, D),
                        ffn2_gate[l].astype(BF16), ffn2_up[l].astype(BF16),
                        ffn2_down[l].astype(BF16), fgain, mod_row=6, final_norm=last, seq=T)
        x = x2d.reshape(B, T, D)
    return x
```

```python
import functools

import jax
import jax.numpy as jnp
from jax import lax
from jax.experimental import pallas as pl
from jax.experimental.pallas import tpu as pltpu

F32 = jnp.float32
BF16 = jnp.bfloat16

D_MODEL = 1024
GDN_HEADS = 4
HEAD_DIM = 128
GDN_WIDTH = GDN_HEADS * HEAD_DIM
POOL_WINDOWS = (2, 4, 8, 16)
POOL_GROUPS = len(POOL_WINDOWS)
POOL_GROUP_DIM = 128
POOL_WIDTH = POOL_GROUPS * POOL_GROUP_DIM
CONV_K = 4
D_FF = 2816
N_MOD = 9
EPS = 1e-6

LANES = 128
SUBLANES = 8
MXU_DIM = 256
VMEM_LIMIT_BYTES = 56 * 1024 * 1024

GDN_CHUNK = LANES
INV_BASE = 16
CONV_HALO = SUBLANES
POOL_HALO = 16
NEG_BIG = -1e30

FFN_ROWS = 1024
FFN_SUBBLOCKS = 4
MIX_ROWS = 512
FFN2_SUB_ROWS = 256
FFN2_SLAB = 3 * MXU_DIM
FFN2_LEAD_PIECES = 5


def _dot(a, b):
    return jnp.dot(a, b, preferred_element_type=F32)


def _silu(x):
    return x * jax.nn.sigmoid(x)


def _norm_mod(x, gain, shift, scale):
    ms = jnp.mean(x * x, axis=-1, keepdims=True)
    y = x * lax.rsqrt(ms + EPS) * gain
    return y * (1.0 + scale) + shift


def _resident(shape):
    nd = len(shape)
    return pl.BlockSpec(shape, lambda *_: (0,) * nd, pipeline_mode=pl.Buffered(1))


def _ada_kernel(c_ref, w_ref, b_ref, o_ref):
    a = _silu(c_ref[...])
    o_ref[...] = jnp.dot(a, w_ref[...], preferred_element_type=F32,
                         precision=lax.Precision.HIGHEST) + b_ref[...]


def _ada_call(c, w_ada, b_ada):
    B = c.shape[0]
    n = w_ada.shape[1]
    bn = D_MODEL
    return pl.pallas_call(
        _ada_kernel,
        grid=(n // bn,),
        in_specs=[
            pl.BlockSpec((B, D_MODEL), lambda j: (0, 0)),
            pl.BlockSpec((D_MODEL, bn), lambda j: (0, j)),
            pl.BlockSpec((1, bn), lambda j: (0, j)),
        ],
        out_specs=pl.BlockSpec((B, bn), lambda j: (0, j)),
        out_shape=jax.ShapeDtypeStruct((B, n), F32),
        compiler_params=pltpu.CompilerParams(dimension_semantics=("arbitrary",)),
        name="adaln_mod",
    )(c, w_ada, b_ada.reshape(1, n))


def _ffn_kernel(x_ref, mod_ref, gain_ref, wg_ref, wu_ref, wd_ref, o_ref):
    x = x_ref[...]
    shift = mod_ref[0, 0:1, :]
    scale = mod_ref[0, 1:2, :]
    gate = mod_ref[0, 2:3, :]
    tm = x.shape[0]
    sub = tm // FFN_SUBBLOCKS
    xs = [x[i * sub:(i + 1) * sub] for i in range(FFN_SUBBLOCKS)]
    hs = [_norm_mod(xi, gain_ref[...], shift, scale).astype(BF16) for xi in xs]
    gus = [(_dot(h, wg_ref[...]), _dot(h, wu_ref[...])) for h in hs]
    acts = [(_silu(g) * u).astype(BF16) for g, u in gus]
    ds = [_dot(a, wd_ref[...]) for a in acts]
    for i, (xi, d) in enumerate(zip(xs, ds)):
        o_ref[i * sub:(i + 1) * sub, :] = xi + (0.5 * gate) * d


def _ffn_call(x2d, mod, gain, wg, wu, wd, *, seq):
    n_rows = x2d.shape[0]
    tm = FFN_ROWS
    steps_per_seq = seq // tm
    return pl.pallas_call(
        _ffn_kernel,
        grid=(n_rows // tm,),
        in_specs=[
            pl.BlockSpec((tm, D_MODEL), lambda i: (i, 0)),
            pl.BlockSpec((1, N_MOD, D_MODEL), lambda i: (i // steps_per_seq, 0, 0)),
            _resident((1, D_MODEL)),
            _resident((D_MODEL, D_FF)),
            _resident((D_MODEL, D_FF)),
            _resident((D_FF, D_MODEL)),
        ],
        out_specs=pl.BlockSpec((tm, D_MODEL), lambda i: (i, 0)),
        out_shape=jax.ShapeDtypeStruct((n_rows, D_MODEL), F32),
        compiler_params=pltpu.CompilerParams(
            dimension_semantics=("arbitrary",), vmem_limit_bytes=VMEM_LIMIT_BYTES),
        name="ffn",
    )(x2d, mod, gain, wg, wu, wd)


def _ffn2_pieces(x, h_all, mod_ref, wg_ref, wu_ref, wd_ref, fgain_ref, o_ref, *, final_norm):
    gate = mod_ref[0, 8:9, :]
    for r0 in range(0, x.shape[0], FFN2_SUB_ROWS):
        xr = x[r0:r0 + FFN2_SUB_ROWS]
        h = h_all[r0:r0 + FFN2_SUB_ROWS]
        acc = None
        for c0 in range(0, D_FF, FFN2_SLAB):
            c1 = min(c0 + FFN2_SLAB, D_FF)
            g = _dot(h, wg_ref[:, c0:c1])
            yield
            u = _dot(h, wu_ref[:, c0:c1])
            yield
            d = _dot((_silu(g) * u).astype(BF16), wd_ref[c0:c1, :])
            acc = d if acc is None else acc + d
            yield
        y = xr + (0.5 * gate) * acc
        if final_norm:
            ms = jnp.mean(y * y, axis=-1, keepdims=True)
            y = y * lax.rsqrt(ms + EPS) * fgain_ref[...]
        o_ref[r0:r0 + FFN2_SUB_ROWS, :] = y


def _inverse_masks(c):
    row = lax.broadcasted_iota(jnp.int32, (c, c), 0)
    col = lax.broadcasted_iota(jnp.int32, (c, c), 1)
    base_shift = INV_BASE.bit_length() - 1
    eye = (row == col).astype(F32)
    same_base = (row >> base_shift) == (col >> base_shift)
    offs = []
    sh = base_shift
    while (1 << sh) < c:
        offs.append(((row >> (sh + 1)) == (col >> (sh + 1))) & ((row >> sh) != (col >> sh)))
        sh += 1
    return eye, same_base, offs


def _tri_inverse_many(ms, masks, between):
    c = ms[0].shape[0]
    eye, same_base, offs = masks
    ds = [jnp.where(same_base, m, 0.0) for m in ms]
    xs = [eye - d for d in ds]
    dbs = [d.astype(BF16) for d in ds]
    ps = [_dot(db, db) for db in dbs]
    between()
    size = 2
    while size < INV_BASE:
        pbs = [p.astype(BF16) for p in ps]
        if 2 * size < INV_BASE:
            xps = [_dot(jnp.concatenate([x.astype(BF16), pb], axis=0), pb)
                   for x, pb in zip(xs, pbs)]
            xs = [x + xp[:c] for x, xp in zip(xs, xps)]
            ps = [xp[c:] for xp in xps]
        else:
            xs = [x + _dot(x.astype(BF16), pb) for x, pb in zip(xs, pbs)]
        between()
        size *= 2
    for off in offs:
        xbs = [x.astype(BF16) for x in xs]
        ys = [_dot(jnp.where(off, m, 0.0).astype(BF16), xb).astype(BF16) for m, xb in zip(ms, xbs)]
        between()
        xs = [x - _dot(xb, y) for x, xb, y in zip(xs, xbs, ys)]
        between()
    return xs


def _mixer_ffn2_kernel(x_ref, mod_ref, mod2_ref, gain_ref, wqkvz_ref, wba_ref, wp_ref, convw_ref,
                       hp_ref, gnorm_ref, poolw_ref, pscale_ref, wout_ref,
                       gain2_ref, wg_ref, wu_ref, wd_ref, fgain_ref,
                       o_ref,
                       s_ref, conv_ref, poolh_ref, wcomb_ref, x2_ref, h2_ref, *,
                       blocks_per_seq, final_norm):
    tm = x_ref.shape[0]
    step = pl.program_id(0)
    last_block = pl.num_programs(0) - 2
    t_blk = lax.rem(jnp.minimum(step, last_block), blocks_per_seq)

    @pl.when(step == 0)
    def _():
        x2_ref[...] = jnp.zeros_like(x2_ref)
        h2_ref[...] = jnp.zeros_like(h2_ref)
        wcomb_ref[0:GDN_WIDTH, :] = wout_ref[0:GDN_WIDTH, :]
        for gi in range(POOL_GROUPS):
            lo, hi = gi * POOL_GROUP_DIM, (gi + 1) * POOL_GROUP_DIM
            scaled = (poolw_ref[gi] * pscale_ref[:, lo:hi]).astype(BF16)
            wcomb_ref[GDN_WIDTH + lo:GDN_WIDTH + hi, :] = _dot(
                scaled, wout_ref[GDN_WIDTH + lo:GDN_WIDTH + hi, :]).astype(BF16)

    @pl.when(t_blk == 0)
    def _():
        s_ref[...] = jnp.zeros_like(s_ref)
        conv_ref[0:CONV_HALO, :] = jnp.zeros((CONV_HALO, conv_ref.shape[1]), F32)
        poolh_ref[...] = jnp.zeros_like(poolh_ref)

    @pl.when(t_blk > 0)
    def _():
        conv_ref[0:CONV_HALO, :] = conv_ref[tm:tm + CONV_HALO, :]

    pieces = _ffn2_pieces(x2_ref[...], h2_ref[...], mod2_ref, wg_ref, wu_ref, wd_ref, fgain_ref,
                          o_ref, final_norm=final_norm)

    def fill(n=1):
        for _ in range(n):
            next(pieces, None)

    stage_calls = [0]

    def fill_every_other_stage():
        stage_calls[0] += 1
        if stage_calls[0] % 2 == 0:
            fill()

    fill(FFN2_LEAD_PIECES)
    x = x_ref[...]
    shift = mod_ref[0, 3:4, :]
    scale = mod_ref[0, 4:5, :]
    gate = mod_ref[0, 5:6, :]
    h = _norm_mod(x, gain_ref[...], shift, scale).astype(BF16)

    ba = _dot(h, wba_ref[...])
    conv_ref[CONV_HALO:, :] = _dot(h, wqkvz_ref[:, :3 * GDN_WIDTH])
    fill(2)
    z = _dot(h, wqkvz_ref[:, 3 * GDN_WIDTH:])
    fill(2)
    p = _dot(h, wp_ref[...])
    fill(2)

    beta_all = jax.nn.sigmoid(ba)
    a_shift = ba + hp_ref[1:2, :]
    softplus = jnp.maximum(a_shift, 0.0) + jnp.log1p(jnp.exp(-jnp.abs(a_shift)))
    g_all = -jnp.exp(hp_ref[0:1, :]) * softplus
    rows = lax.broadcasted_iota(jnp.int32, (tm, 1), 0)
    row_in_chunk = rows & (GDN_CHUNK - 1)
    gc = g_all
    shift_rows = 1
    while shift_rows < GDN_CHUNK:
        gc = gc + jnp.where(row_in_chunk >= shift_rows, pltpu.roll(gc, shift_rows, axis=0), 0.0)
        shift_rows *= 2
    e_g_all = jnp.exp(gc)
    gc_t = gc.T[:SUBLANES]

    n_slabs = (CONV_HALO + tm) // SUBLANES
    slabs = conv_ref[...].reshape(n_slabs, SUBLANES, 3 * GDN_WIDTH)
    first_row = lax.broadcasted_iota(jnp.int32, (1, SUBLANES, 1), 1) == 0
    acc = slabs * convw_ref[0:1, :].reshape(1, 1, 3 * GDN_WIDTH)
    for j in range(1, CONV_K):
        rot = pltpu.roll(acc, 1, axis=1)
        above = jnp.concatenate([rot[:1], rot[:-1]], axis=0)
        acc = (jnp.where(first_row, above, rot)
               + slabs * convw_ref[j:j + 1, :].reshape(1, 1, 3 * GDN_WIDTH))
    qkv = _silu(acc[1:]).reshape(tm, 3 * GDN_WIDTH)

    ri = lax.broadcasted_iota(jnp.int32, (GDN_CHUNK, GDN_CHUNK), 0)
    ci = lax.broadcasted_iota(jnp.int32, (GDN_CHUNK, GDN_CHUNK), 1)
    causal = ri >= ci
    strict = ri > ci
    gnorm = gnorm_ref[...]
    inv_masks = _inverse_masks(GDN_CHUNK)

    n_chunks = tm // GDN_CHUNK
    heads = range(GDN_HEADS)
    q_hs, k_hs, kt_hs, v_hs = [], [], [], []
    for hd in heads:
        lo, hi = hd * HEAD_DIM, (hd + 1) * HEAD_DIM
        q_h = qkv[:, lo:hi]
        k_h = qkv[:, GDN_WIDTH + lo:GDN_WIDTH + hi]
        q_hs.append(q_h * (lax.rsqrt(jnp.sum(q_h * q_h, axis=-1, keepdims=True) + EPS)
                           * HEAD_DIM ** -0.5))
        k_h = k_h * lax.rsqrt(jnp.sum(k_h * k_h, axis=-1, keepdims=True) + EPS)
        k_hs.append(k_h)
        kt_hs.append(k_h.T)
        v_hs.append(qkv[:, 2 * GDN_WIDTH + lo:2 * GDN_WIDTH + hi])

    probs = [(c, hd) for c in range(n_chunks) for hd in heads]
    decays, e_gs, kt_decs, e_lasts, kbs, kqs = [], [], [], [], [], []
    for c, hd in probs:
        r0, r1 = c * GDN_CHUNK, (c + 1) * GDN_CHUNK
        lane = GDN_HEADS + hd
        g_col = gc[r0:r1, lane:lane + 1]
        g_row = gc_t[lane:lane + 1, r0:r1]
        g_last = gc_t[lane:lane + 1, r1 - 1:r1]
        decays.append(jnp.exp(jnp.where(causal, g_col - g_row, NEG_BIG)))
        e_gs.append(e_g_all[r0:r1, lane:lane + 1])
        e_lasts.append(jnp.exp(g_last))
        kt_c = kt_hs[hd][:, r0:r1]
        kt_decs.append((kt_c * jnp.exp(g_last - g_row)).astype(BF16))
        kb = k_hs[hd][r0:r1] * beta_all[r0:r1, hd:hd + 1]
        kbs.append(kb)
        kqs.append(_dot(jnp.concatenate([kb, q_hs[hd][r0:r1]], axis=0).astype(BF16),
                        kt_c.astype(BF16)))
    fill()
    ms = [jnp.where(strict, kq[:GDN_CHUNK] * d, 0.0) for kq, d in zip(kqs, decays)]
    intras = [(kq[GDN_CHUNK:] * d).astype(BF16) for kq, d in zip(kqs, decays)]
    t_invs = _tri_inverse_many(ms, inv_masks, fill_every_other_stage)
    uws = []
    for i, (c, hd) in enumerate(probs):
        r0, r1 = c * GDN_CHUNK, (c + 1) * GDN_CHUNK
        vb = v_hs[hd][r0:r1] * beta_all[r0:r1, hd:hd + 1]
        uws.append(_dot(t_invs[i].astype(BF16),
                        jnp.concatenate([vb, kbs[i] * e_gs[i]], axis=1).astype(BF16)))

    t_glob = (t_blk * tm + rows + 1).astype(F32)

    def pool_and_project(c, gdn_outs, p_c, p_tail):
        r0, r1 = c * GDN_CHUNK, (c + 1) * GDN_CHUNK
        ext = jnp.concatenate([p_tail, p_c], axis=0)
        pool_outs = []
        for gi, win in enumerate(POOL_WINDOWS):
            lo, hi = gi * POOL_GROUP_DIM, (gi + 1) * POOL_GROUP_DIM
            wsum = ext[:, lo:hi]
            sh = 1
            while sh < win:
                wsum = wsum + pltpu.roll(wsum, sh, axis=0)
                sh *= 2
            cnt = jnp.minimum(t_glob[r0:r1], float(win))
            pool_outs.append(wsum[POOL_HALO:] / cnt - p_c[:, lo:hi])
        cat = jnp.concatenate(gdn_outs + pool_outs, axis=1).astype(BF16)
        mixed = _dot(cat, wcomb_ref[...])
        x2_c = x[r0:r1] + gate * mixed
        x2_ref[r0:r1, :] = x2_c
        h2_ref[r0:r1, :] = _norm_mod(x2_c, gain2_ref[...], mod_ref[0, 6:7, :],
                                     mod_ref[0, 7:8, :]).astype(BF16)

    states = [s_ref[hd] for hd in heads]
    p_prev, o_prev, p_tail = None, None, poolh_ref[...]
    for c in range(n_chunks):
        r0, r1 = c * GDN_CHUNK, (c + 1) * GDN_CHUNK
        idx = [c * GDN_HEADS + hd for hd in heads]
        s_bfs = [st.astype(BF16) for st in states]
        ws = [_dot(uws[i][:, HEAD_DIM:].astype(BF16), s_bfs[hd]) for hd, i in zip(heads, idx)]
        fill()
        z_c = z[r0:r1]
        p_c = p[r0:r1]
        v_news = [(uws[i][:, :HEAD_DIM] - w).astype(BF16) for i, w in zip(idx, ws)]
        new_states = [states[hd] * e_lasts[i] + _dot(kt_decs[i], v_news[hd])
                      for hd, i in zip(heads, idx)]
        o_c = []
        for hd, i in zip(heads, idx):
            lo, hi = hd * HEAD_DIM, (hd + 1) * HEAD_DIM
            q_dec = q_hs[hd][r0:r1] * e_gs[i]
            o = _dot(jnp.concatenate([q_dec.astype(BF16), intras[i]], axis=1),
                     jnp.concatenate([s_bfs[hd], v_news[hd]], axis=0))
            o = o * lax.rsqrt(jnp.mean(o * o, axis=-1, keepdims=True) + EPS) * gnorm
            o_c.append(o * _silu(z_c[:, lo:hi]))
        fill()
        if c > 0:
            pool_and_project(c - 1, o_prev, p_prev, p_tail)
            p_tail = p_prev[GDN_CHUNK - POOL_HALO:]
        p_prev, o_prev = p_c, o_c
        states = new_states
    pool_and_project(n_chunks - 1, o_prev, p_prev, p_tail)
    poolh_ref[...] = p_prev[GDN_CHUNK - POOL_HALO:]
    for hd in heads:
        s_ref[hd] = states[hd]
    for _ in pieces:
        pass


def _mixer_ffn2_call(x2d, mod, gain, wqkvz, wba, wp, convw, hp, gnorm, poolw, pscale, wout,
                     gain2, wg, wu, wd, fgain, *, seq, final_norm):
    n_rows = x2d.shape[0]
    tm = MIX_ROWS
    n_blocks = n_rows // tm
    blocks_per_seq = seq // tm
    last = n_blocks - 1
    kern = functools.partial(_mixer_ffn2_kernel, blocks_per_seq=blocks_per_seq,
                             final_norm=final_norm)
    weights = (gain, wqkvz, wba, wp, convw, hp, gnorm, poolw, pscale, wout, gain2, wg, wu, wd, fgain)
    return pl.pallas_call(
        kern,
        grid=(n_blocks + 1,),
        in_specs=[
            pl.BlockSpec((tm, D_MODEL), lambda i: (jnp.minimum(i, last), 0)),
            pl.BlockSpec((1, N_MOD, D_MODEL),
                         lambda i: (jnp.minimum(i, last) // blocks_per_seq, 0, 0)),
            pl.BlockSpec((1, N_MOD, D_MODEL),
                         lambda i: (jnp.maximum(i - 1, 0) // blocks_per_seq, 0, 0)),
        ] + [_resident(w.shape) for w in weights],
        out_specs=pl.BlockSpec((tm, D_MODEL), lambda i: (jnp.maximum(i - 1, 0), 0)),
        out_shape=jax.ShapeDtypeStruct((n_rows, D_MODEL), F32),
        scratch_shapes=[
            pltpu.VMEM((GDN_HEADS, HEAD_DIM, HEAD_DIM), F32),
            pltpu.VMEM((CONV_HALO + tm, 3 * GDN_WIDTH), F32),
            pltpu.VMEM((POOL_HALO, POOL_WIDTH), F32),
            pltpu.VMEM((D_MODEL, D_MODEL), BF16),
            pltpu.VMEM((tm, D_MODEL), F32),
            pltpu.VMEM((tm, D_MODEL), BF16),
        ],
        compiler_params=pltpu.CompilerParams(
            dimension_semantics=("arbitrary",), vmem_limit_bytes=VMEM_LIMIT_BYTES),
        name="mixer_ffn2",
    )(x2d, mod, mod, *weights)


def kernel(x, c, w_ada, b_ada, norm_ffn1, ffn1_gate, ffn1_up, ffn1_down, norm_mix, w_in, conv_w,
           a_log, dt_bias, gdn_norm, pool_w, pool_scale, w_out, norm_ffn2, ffn2_gate, ffn2_up,
           ffn2_down, final_norm):
    B, T, D = x.shape
    depth = w_ada.shape[0]
    H, GW = GDN_HEADS, GDN_WIDTH
    fgain = final_norm.reshape(1, D)
    x2d = x.reshape(B * T, D)
    for l in range(depth):
        mod = _ada_call(c, w_ada[l], b_ada[l]).reshape(B, N_MOD, D)

        x2d = _ffn_call(x2d, mod, norm_ffn1[l].reshape(1, D), ffn1_gate[l].astype(BF16),
                        ffn1_up[l].astype(BF16), ffn1_down[l].astype(BF16), seq=T)

        wi = w_in[l]
        wqkvz = wi[:, :4 * GW].astype(BF16)
        wba = jnp.pad(wi[:, 4 * GW:4 * GW + 2 * H], ((0, 0), (0, LANES - 2 * H))).astype(BF16)
        wp = wi[:, 4 * GW + 2 * H:].astype(BF16)
        hp = jnp.zeros((2, LANES), F32)
        hp = hp.at[0, H:2 * H].set(a_log[l]).at[1, H:2 * H].set(dt_bias[l])
        x2d = _mixer_ffn2_call(
            x2d, mod, norm_mix[l].reshape(1, D), wqkvz, wba, wp, conv_w[l], hp,
            gdn_norm[l].reshape(1, HEAD_DIM), pool_w[l], pool_scale[l].reshape(1, POOL_WIDTH),
            w_out[l].astype(BF16), norm_ffn2[l].reshape(1, D), ffn2_gate[l].astype(BF16),
            ffn2_up[l].astype(BF16), ffn2_down[l].astype(BF16), fgain,
            seq=T, final_norm=(l == depth - 1))
    return x2d.reshape(B, T, D)
```

```python
import functools

import jax
import jax.numpy as jnp
from jax import lax
from jax.experimental import pallas as pl
from jax.experimental.pallas import tpu as pltpu

F32 = jnp.float32
BF16 = jnp.bfloat16

D_MODEL = 1024
GDN_HEADS = 4
HEAD_DIM = 128
GDN_WIDTH = GDN_HEADS * HEAD_DIM
POOL_WINDOWS = (2, 4, 8, 16)
POOL_GROUPS = len(POOL_WINDOWS)
POOL_GROUP_DIM = 128
POOL_WIDTH = POOL_GROUPS * POOL_GROUP_DIM
CONV_K = 4
D_FF = 2816
N_MOD = 9
EPS = 1e-6

LANES = 128
SUBLANES = 8
MXU_DIM = 256
VMEM_LIMIT_BYTES = 56 * 1024 * 1024

GDN_CHUNK = LANES
INV_BASE = 16
CONV_HALO = SUBLANES
POOL_HALO = 16
NEG_BIG = -1e30

FFN_ROWS = 1024
FFN_SUBBLOCKS = 4
MIX_ROWS = 512
FFN2_SUB_ROWS = 256
FFN2_SLAB = 3 * MXU_DIM
FFN2_LEAD_PIECES = 5


def _dot(a, b):
    return jnp.dot(a, b, preferred_element_type=F32)


def _silu(x):
    return x * jax.nn.sigmoid(x)


def _norm_mod(x, gain, shift, scale):
    ms = jnp.mean(x * x, axis=-1, keepdims=True)
    y = x * lax.rsqrt(ms + EPS) * gain
    return y * (1.0 + scale) + shift


def _resident(shape):
    nd = len(shape)
    return pl.BlockSpec(shape, lambda *_: (0,) * nd, pipeline_mode=pl.Buffered(1))


def _ada_kernel(c_ref, w_ref, b_ref, o_ref):
    a = _silu(c_ref[...])
    o_ref[...] = jnp.dot(a, w_ref[...], preferred_element_type=F32,
                         precision=lax.Precision.HIGHEST) + b_ref[...]


def _ada_call(c, w_ada, b_ada):
    B = c.shape[0]
    n = w_ada.shape[1]
    bn = D_MODEL
    return pl.pallas_call(
        _ada_kernel,
        grid=(n // bn,),
        in_specs=[
            pl.BlockSpec((B, D_MODEL), lambda j: (0, 0)),
            pl.BlockSpec((D_MODEL, bn), lambda j: (0, j)),
            pl.BlockSpec((1, bn), lambda j: (0, j)),
        ],
        out_specs=pl.BlockSpec((B, bn), lambda j: (0, j)),
        out_shape=jax.ShapeDtypeStruct((B, n), F32),
        compiler_params=pltpu.CompilerParams(dimension_semantics=("arbitrary",)),
        name="adaln_mod",
    )(c, w_ada, b_ada.reshape(1, n))


def _ffn_kernel(x_ref, mod_ref, gain_ref, wg_ref, wu_ref, wd_ref, o_ref):
    x = x_ref[...]
    shift = mod_ref[0, 0:1, :]
    scale = mod_ref[0, 1:2, :]
    gate = mod_ref[0, 2:3, :]
    tm = x.shape[0]
    sub = tm // FFN_SUBBLOCKS
    xs = [x[i * sub:(i + 1) * sub] for i in range(FFN_SUBBLOCKS)]
    hs = [_norm_mod(xi, gain_ref[...], shift, scale).astype(BF16) for xi in xs]
    gus = [(_dot(h, wg_ref[...]), _dot(h, wu_ref[...])) for h in hs]
    acts = [(_silu(g) * u).astype(BF16) for g, u in gus]
    ds = [_dot(a, wd_ref[...]) for a in acts]
    for i, (xi, d) in enumerate(zip(xs, ds)):
        o_ref[i * sub:(i + 1) * sub, :] = xi + (0.5 * gate) * d


def _ffn_call(x2d, mod, gain, wg, wu, wd, *, seq):
    n_rows = x2d.shape[0]
    tm = FFN_ROWS
    steps_per_seq = seq // tm
    return pl.pallas_call(
        _ffn_kernel,
        grid=(n_rows // tm,),
        in_specs=[
            pl.BlockSpec((tm, D_MODEL), lambda i: (i, 0)),
            pl.BlockSpec((1, N_MOD, D_MODEL), lambda i: (i // steps_per_seq, 0, 0)),
            _resident((1, D_MODEL)),
            _resident((D_MODEL, D_FF)),
            _resident((D_MODEL, D_FF)),
            _resident((D_FF, D_MODEL)),
        ],
        out_specs=pl.BlockSpec((tm, D_MODEL), lambda i: (i, 0)),
        out_shape=jax.ShapeDtypeStruct((n_rows, D_MODEL), F32),
        compiler_params=pltpu.CompilerParams(
            dimension_semantics=("arbitrary",), vmem_limit_bytes=VMEM_LIMIT_BYTES),
        name="ffn",
    )(x2d, mod, gain, wg, wu, wd)


def _ffn2_pieces(x, h_all, mod_ref, wg_ref, wu_ref, wd_ref, fgain_ref, o_ref, *, final_norm):
    gate = mod_ref[0, 8:9, :]
    for r0 in range(0, x.shape[0], FFN2_SUB_ROWS):
        xr = x[r0:r0 + FFN2_SUB_ROWS]
        h = h_all[r0:r0 + FFN2_SUB_ROWS]
        acc = None
        for c0 in range(0, D_FF, FFN2_SLAB):
            c1 = min(c0 + FFN2_SLAB, D_FF)
            g = _dot(h, wg_ref[:, c0:c1])
            yield
            u = _dot(h, wu_ref[:, c0:c1])
            yield
            d = _dot((_silu(g) * u).astype(BF16), wd_ref[c0:c1, :])
            acc = d if acc is None else acc + d
            yield
        y = xr + (0.5 * gate) * acc
        if final_norm:
            ms = jnp.mean(y * y, axis=-1, keepdims=True)
            y = y * lax.rsqrt(ms + EPS) * fgain_ref[...]
        o_ref[r0:r0 + FFN2_SUB_ROWS, :] = y


def _inverse_masks(c):
    row = lax.broadcasted_iota(jnp.int32, (c, c), 0)
    col = lax.broadcasted_iota(jnp.int32, (c, c), 1)
    base_shift = INV_BASE.bit_length() - 1
    eye = (row == col).astype(F32)
    same_base = (row >> base_shift) == (col >> base_shift)
    offs = []
    sh = base_shift
    while (1 << sh) < c:
        offs.append(((row >> (sh + 1)) == (col >> (sh + 1))) & ((row >> sh) != (col >> sh)))
        sh += 1
    return eye, same_base, offs


def _tri_inverse_many(ms, masks, between):
    c = ms[0].shape[0]
    eye, same_base, offs = masks
    ds = [jnp.where(same_base, m, 0.0) for m in ms]
    xs = [eye - d for d in ds]
    dbs = [d.astype(BF16) for d in ds]
    ps = [_dot(db, db) for db in dbs]
    between()
    size = 2
    while size < INV_BASE:
        pbs = [p.astype(BF16) for p in ps]
        if 2 * size < INV_BASE:
            xps = [_dot(jnp.concatenate([x.astype(BF16), pb], axis=0), pb)
                   for x, pb in zip(xs, pbs)]
            xs = [x + xp[:c] for x, xp in zip(xs, xps)]
            ps = [xp[c:] for xp in xps]
        else:
            xs = [x + _dot(x.astype(BF16), pb) for x, pb in zip(xs, pbs)]
        between()
        size *= 2
    for off in offs:
        xbs = [x.astype(BF16) for x in xs]
        ys = [_dot(jnp.where(off, m, 0.0).astype(BF16), xb).astype(BF16) for m, xb in zip(ms, xbs)]
        between()
        xs = [x - _dot(xb, y) for x, xb, y in zip(xs, xbs, ys)]
        between()
    return xs


def _mixer_ffn2_kernel(x_ref, mod_ref, mod2_ref, gain_ref, wqkvz_ref, wba_ref, wp_ref, convw_ref,
                       hp_ref, gnorm_ref, poolw_ref, pscale_ref, wout_ref,
                       gain2_ref, wg_ref, wu_ref, wd_ref, fgain_ref,
                       o_ref,
                       s_ref, conv_ref, poolh_ref, wcomb_ref, x2_ref, h2_ref, *,
                       blocks_per_seq, final_norm):
    tm = x_ref.shape[0]
    step = pl.program_id(0)
    last_block = pl.num_programs(0) - 2
    t_blk = lax.rem(jnp.minimum(step, last_block), blocks_per_seq)

    @pl.when(step == 0)
    def _():
        x2_ref[...] = jnp.zeros_like(x2_ref)
        h2_ref[...] = jnp.zeros_like(h2_ref)
        wcomb_ref[0:GDN_WIDTH, :] = wout_ref[0:GDN_WIDTH, :]
        for gi in range(POOL_GROUPS):
            lo, hi = gi * POOL_GROUP_DIM, (gi + 1) * POOL_GROUP_DIM
            scaled = (poolw_ref[gi] * pscale_ref[:, lo:hi]).astype(BF16)
            wcomb_ref[GDN_WIDTH + lo:GDN_WIDTH + hi, :] = _dot(
                scaled, wout_ref[GDN_WIDTH + lo:GDN_WIDTH + hi, :]).astype(BF16)

    @pl.when(t_blk == 0)
    def _():
        s_ref[...] = jnp.zeros_like(s_ref)
        conv_ref[0:CONV_HALO, :] = jnp.zeros((CONV_HALO, conv_ref.shape[1]), F32)
        poolh_ref[...] = jnp.zeros_like(poolh_ref)

    @pl.when(t_blk > 0)
    def _():
        conv_ref[0:CONV_HALO, :] = conv_ref[tm:tm + CONV_HALO, :]

    pieces = _ffn2_pieces(x2_ref[...], h2_ref[...], mod2_ref, wg_ref, wu_ref, wd_ref, fgain_ref,
                          o_ref, final_norm=final_norm)

    def fill(n=1):
        for _ in range(n):
            next(pieces, None)

    stage_calls = [0]

    def fill_every_other_stage():
        stage_calls[0] += 1
        if stage_calls[0] % 2 == 0:
            fill()

    fill(FFN2_LEAD_PIECES)
    shift = mod_ref[0, 3:4, :]
    scale = mod_ref[0, 4:5, :]
    gate = mod_ref[0, 5:6, :]
    h = _norm_mod(x_ref[...], gain_ref[...], shift, scale).astype(BF16)

    ba = _dot(h, wba_ref[...])
    conv_ref[CONV_HALO:, :] = _dot(h, wqkvz_ref[:, :3 * GDN_WIDTH])
    fill(2)
    z = _dot(h, wqkvz_ref[:, 3 * GDN_WIDTH:])
    fill(2)
    p = _dot(h, wp_ref[...])
    fill(2)

    beta_all = jax.nn.sigmoid(ba)
    a_shift = ba + hp_ref[1:2, :]
    softplus = jnp.maximum(a_shift, 0.0) + jnp.log1p(jnp.exp(-jnp.abs(a_shift)))
    g_all = -jnp.exp(hp_ref[0:1, :]) * softplus
    rows = lax.broadcasted_iota(jnp.int32, (tm, 1), 0)
    row_in_chunk = rows & (GDN_CHUNK - 1)
    gc = g_all
    shift_rows = 1
    while shift_rows < GDN_CHUNK:
        gc = gc + jnp.where(row_in_chunk >= shift_rows, pltpu.roll(gc, shift_rows, axis=0), 0.0)
        shift_rows *= 2
    e_g_all = jnp.exp(gc)
    gc_t = gc.T[:SUBLANES]

    n_slabs = (CONV_HALO + tm) // SUBLANES
    slabs = conv_ref[...].reshape(n_slabs, SUBLANES, 3 * GDN_WIDTH)
    first_row = lax.broadcasted_iota(jnp.int32, (1, SUBLANES, 1), 1) == 0
    acc = slabs * convw_ref[0:1, :].reshape(1, 1, 3 * GDN_WIDTH)
    for j in range(1, CONV_K):
        rot = pltpu.roll(acc, 1, axis=1)
        above = jnp.concatenate([rot[:1], rot[:-1]], axis=0)
        acc = (jnp.where(first_row, above, rot)
               + slabs * convw_ref[j:j + 1, :].reshape(1, 1, 3 * GDN_WIDTH))
    qkv = _silu(acc[1:]).reshape(tm, 3 * GDN_WIDTH)

    ri = lax.broadcasted_iota(jnp.int32, (GDN_CHUNK, GDN_CHUNK), 0)
    ci = lax.broadcasted_iota(jnp.int32, (GDN_CHUNK, GDN_CHUNK), 1)
    causal = ri >= ci
    strict = ri > ci
    gnorm = gnorm_ref[...]
    inv_masks = _inverse_masks(GDN_CHUNK)

    n_chunks = tm // GDN_CHUNK
    heads = range(GDN_HEADS)
    q_hs, k_hs, kt_hs, v_hs = [], [], [], []
    for hd in heads:
        lo, hi = hd * HEAD_DIM, (hd + 1) * HEAD_DIM
        q_h = qkv[:, lo:hi]
        k_h = qkv[:, GDN_WIDTH + lo:GDN_WIDTH + hi]
        q_hs.append(q_h * (lax.rsqrt(jnp.sum(q_h * q_h, axis=-1, keepdims=True) + EPS)
                           * HEAD_DIM ** -0.5))
        k_h = k_h * lax.rsqrt(jnp.sum(k_h * k_h, axis=-1, keepdims=True) + EPS)
        k_hs.append(k_h)
        kt_hs.append(k_h.T)
        v_hs.append(qkv[:, 2 * GDN_WIDTH + lo:2 * GDN_WIDTH + hi])

    probs = [(c, hd) for c in range(n_chunks) for hd in heads]
    decays, e_gs, kt_decs, e_lasts, kbs, kqs = [], [], [], [], [], []
    for c, hd in probs:
        r0, r1 = c * GDN_CHUNK, (c + 1) * GDN_CHUNK
        lane = GDN_HEADS + hd
        g_col = gc[r0:r1, lane:lane + 1]
        g_row = gc_t[lane:lane + 1, r0:r1]
        g_last = gc_t[lane:lane + 1, r1 - 1:r1]
        decays.append(jnp.exp(jnp.where(causal, g_col - g_row, NEG_BIG)))
        e_gs.append(e_g_all[r0:r1, lane:lane + 1])
        e_lasts.append(jnp.exp(g_last))
        kt_c = kt_hs[hd][:, r0:r1]
        kt_decs.append((kt_c * jnp.exp(g_last - g_row)).astype(BF16))
        kb = k_hs[hd][r0:r1] * beta_all[r0:r1, hd:hd + 1]
        kbs.append(kb)
        kqs.append(_dot(jnp.concatenate([kb, q_hs[hd][r0:r1]], axis=0).astype(BF16),
                        kt_c.astype(BF16)))
    fill()
    ms = [jnp.where(strict, kq[:GDN_CHUNK] * d, 0.0) for kq, d in zip(kqs, decays)]
    intras = [(kq[GDN_CHUNK:] * d).astype(BF16) for kq, d in zip(kqs, decays)]
    t_invs = _tri_inverse_many(ms, inv_masks, fill_every_other_stage)
    uws = []
    for i, (c, hd) in enumerate(probs):
        r0, r1 = c * GDN_CHUNK, (c + 1) * GDN_CHUNK
        vb = v_hs[hd][r0:r1] * beta_all[r0:r1, hd:hd + 1]
        uws.append(_dot(t_invs[i].astype(BF16),
                        jnp.concatenate([vb, kbs[i] * e_gs[i]], axis=1).astype(BF16)))

    t_glob = (t_blk * tm + rows + 1).astype(F32)

    def pool_and_project(c, gdn_outs, p_c, p_tail):
        r0, r1 = c * GDN_CHUNK, (c + 1) * GDN_CHUNK
        ext = jnp.concatenate([p_tail, p_c], axis=0)
        pool_outs = []
        for gi, win in enumerate(POOL_WINDOWS):
            lo, hi = gi * POOL_GROUP_DIM, (gi + 1) * POOL_GROUP_DIM
            wsum = ext[:, lo:hi]
            sh = 1
            while sh < win:
                wsum = wsum + pltpu.roll(wsum, sh, axis=0)
                sh *= 2
            cnt = jnp.minimum(t_glob[r0:r1], float(win))
            pool_outs.append(wsum[POOL_HALO:] / cnt - p_c[:, lo:hi])
        cat = jnp.concatenate(gdn_outs + pool_outs, axis=1).astype(BF16)
        mixed = _dot(cat, wcomb_ref[...])
        x2_c = x_ref[r0:r1, :] + gate * mixed
        x2_ref[r0:r1, :] = x2_c
        h2_ref[r0:r1, :] = _norm_mod(x2_c, gain2_ref[...], mod_ref[0, 6:7, :],
                                     mod_ref[0, 7:8, :]).astype(BF16)

    states = [s_ref[hd] for hd in heads]
    p_prev, o_prev, p_tail = None, None, poolh_ref[...]
    for c in range(n_chunks):
        r0, r1 = c * GDN_CHUNK, (c + 1) * GDN_CHUNK
        idx = [c * GDN_HEADS + hd for hd in heads]
        s_bfs = [st.astype(BF16) for st in states]
        ws = [_dot(uws[i][:, HEAD_DIM:].astype(BF16), s_bfs[hd]) for hd, i in zip(heads, idx)]
        fill()
        z_c = z[r0:r1]
        p_c = p[r0:r1]
        v_news = [(uws[i][:, :HEAD_DIM] - w).astype(BF16) for i, w in zip(idx, ws)]
        new_states = [states[hd] * e_lasts[i] + _dot(kt_decs[i], v_news[hd])
                      for hd, i in zip(heads, idx)]
        o_c = []
        for hd, i in zip(heads, idx):
            lo, hi = hd * HEAD_DIM, (hd + 1) * HEAD_DIM
            q_dec = q_hs[hd][r0:r1] * e_gs[i]
            o = _dot(jnp.concatenate([q_dec.astype(BF16), intras[i]], axis=1),
                     jnp.concatenate([s_bfs[hd], v_news[hd]], axis=0))
            o = o * lax.rsqrt(jnp.mean(o * o, axis=-1, keepdims=True) + EPS) * gnorm
            o_c.append(o * _silu(z_c[:, lo:hi]))
        fill()
        if c > 0:
            pool_and_project(c - 1, o_prev, p_prev, p_tail)
            p_tail = p_prev[GDN_CHUNK - POOL_HALO:]
        p_prev, o_prev = p_c, o_c
        states = new_states
    pool_and_project(n_chunks - 1, o_prev, p_prev, p_tail)
    poolh_ref[...] = p_prev[GDN_CHUNK - POOL_HALO:]
    for hd in heads:
        s_ref[hd] = states[hd]
    for _ in pieces:
        pass


def _mixer_ffn2_call(x2d, mod, gain, wqkvz, wba, wp, convw, hp, gnorm, poolw, pscale, wout,
                     gain2, wg, wu, wd, fgain, *, seq, final_norm):
    n_rows = x2d.shape[0]
    tm = MIX_ROWS
    n_blocks = n_rows // tm
    blocks_per_seq = seq // tm
    last = n_blocks - 1
    kern = functools.partial(_mixer_ffn2_kernel, blocks_per_seq=blocks_per_seq,
                             final_norm=final_norm)
    weights = (gain, wqkvz, wba, wp, convw, hp, gnorm, poolw, pscale, wout, gain2, wg, wu, wd, fgain)
    return pl.pallas_call(
        kern,
        grid=(n_blocks + 1,),
        in_specs=[
            pl.BlockSpec((tm, D_MODEL), lambda i: (jnp.minimum(i, last), 0)),
            pl.BlockSpec((1, N_MOD, D_MODEL),
                         lambda i: (jnp.minimum(i, last) // blocks_per_seq, 0, 0)),
            pl.BlockSpec((1, N_MOD, D_MODEL),
                         lambda i: (jnp.maximum(i - 1, 0) // blocks_per_seq, 0, 0)),
        ] + [_resident(w.shape) for w in weights],
        out_specs=pl.BlockSpec((tm, D_MODEL), lambda i: (jnp.maximum(i - 1, 0), 0)),
        out_shape=jax.ShapeDtypeStruct((n_rows, D_MODEL), F32),
        scratch_shapes=[
            pltpu.VMEM((GDN_HEADS, HEAD_DIM, HEAD_DIM), F32),
            pltpu.VMEM((CONV_HALO + tm, 3 * GDN_WIDTH), F32),
            pltpu.VMEM((POOL_HALO, POOL_WIDTH), F32),
            pltpu.VMEM((D_MODEL, D_MODEL), BF16),
            pltpu.VMEM((tm, D_MODEL), F32),
            pltpu.VMEM((tm, D_MODEL), BF16),
        ],
        compiler_params=pltpu.CompilerParams(
            dimension_semantics=("arbitrary",), vmem_limit_bytes=VMEM_LIMIT_BYTES),
        name="mixer_ffn2",
    )(x2d, mod, mod, *weights)


def kernel(x, c, w_ada, b_ada, norm_ffn1, ffn1_gate, ffn1_up, ffn1_down, norm_mix, w_in, conv_w,
           a_log, dt_bias, gdn_norm, pool_w, pool_scale, w_out, norm_ffn2, ffn2_gate, ffn2_up,
           ffn2_down, final_norm):
    B, T, D = x.shape
    depth = w_ada.shape[0]
    H, GW = GDN_HEADS, GDN_WIDTH
    fgain = final_norm.reshape(1, D)
    x2d = x.reshape(B * T, D)
    for l in range(depth):
        mod = _ada_call(c, w_ada[l], b_ada[l]).reshape(B, N_MOD, D)

        x2d = _ffn_call(x2d, mod, norm_ffn1[l].reshape(1, D), ffn1_gate[l].astype(BF16),
                        ffn1_up[l].astype(BF16), ffn1_down[l].astype(BF16), seq=T)

        wi = w_in[l]
        wqkvz = wi[:, :4 * GW].astype(BF16)
        wba = jnp.pad(wi[:, 4 * GW:4 * GW + 2 * H], ((0, 0), (0, LANES - 2 * H))).astype(BF16)
        wp = wi[:, 4 * GW + 2 * H:].astype(BF16)
        hp = jnp.zeros((2, LANES), F32)
        hp = hp.at[0, H:2 * H].set(a_log[l]).at[1, H:2 * H].set(dt_bias[l])
        x2d = _mixer_ffn2_call(
            x2d, mod, norm_mix[l].reshape(1, D), wqkvz, wba, wp, conv_w[l], hp,
            gdn_norm[l].reshape(1, HEAD_DIM), pool_w[l], pool_scale[l].reshape(1, POOL_WIDTH),
            w_out[l].astype(BF16), norm_ffn2[l].reshape(1, D), ffn2_gate[l].astype(BF16),
            ffn2_up[l].astype(BF16), ffn2_down[l].astype(BF16), fgain,
            seq=T, final_norm=(l == depth - 1))
    return x2d.reshape(B, T, D)
```
